```python
import jax, jax.numpy as jnp
from jax import lax
import numpy as np

D_MODEL = 1024
BATCH = 8
SEQ = 4096
DEPTH = 2
DEC_BATCH = 32
DEC_SEQ = 1
PAST_LEN = 16384
PAGE_SIZE = 128

N_A = DEPTH // 2
N_B = DEPTH - N_A
CONV_EXPAND = 2
D_CONV = CONV_EXPAND * D_MODEL
CONV_WIDTH = 3
N_HEADS = 16
HEAD_DIM = D_MODEL // N_HEADS
KV_HEADS = 4
GROUP = N_HEADS // KV_HEADS
D_ATTN = N_HEADS * HEAD_DIM
N_BRANCH = 3
CMP_STRIDE = 16
CMP_LEN = 2 * CMP_STRIDE
CMP_HIDDEN = HEAD_DIM
SEL_BLOCK = 64
SEL_TOPK = 16
WINDOW = 512
Q_BLOCK = 128
EPS = 1e-6
NEG = -1e30
FORCE = 1e9
SCALE = HEAD_DIM ** -0.5

kernel_name = 'yoco_shortconv_nsa_step'


def rmsnorm(x, g):
    xf = x.astype(jnp.float32)
    y = xf * lax.rsqrt(jnp.mean(xf * xf, axis=-1, keepdims=True) + EPS)
    return y.astype(x.dtype) * g


def masked_softmax(s, mask):
    s = jnp.where(mask, s, NEG)
    e = jnp.exp(s - jnp.max(s, axis=-1, keepdims=True)) * mask
    return e / jnp.maximum(jnp.sum(e, axis=-1, keepdims=True), 1e-30)


def short_conv_layer(xn, buf, w_in, conv_w, w_out):
    b_gate, c_gate, h, z = jnp.split(xn @ w_in, 4, axis=-1)
    u = c_gate * h
    ext = jnp.concatenate([buf.astype(u.dtype), u], axis=1)
    T = u.shape[1]
    v = sum(conv_w[j] * ext[:, j:j + T] for j in range(CONV_WIDTH))
    y = (jax.nn.silu(z) * b_gate * v) @ w_out
    return y, ext[:, T:]


def shared_kv(h, norm_kv, w_kv):
    kv = rmsnorm(h, norm_kv) @ w_kv
    return kv.reshape(h.shape[0], h.shape[1], N_BRANCH, 2, KV_HEADS, HEAD_DIM)


def query_side(xn, w_qg):
    B, T = xn.shape[:2]
    proj = xn @ w_qg
    q = proj[..., :D_ATTN].reshape(B, T, N_HEADS, HEAD_DIM)
    gate = jax.nn.sigmoid(proj[..., D_ATTN:D_ATTN + N_BRANCH * N_HEADS]).reshape(B, T, N_HEADS, N_BRANCH)
    z = proj[..., D_ATTN + N_BRANCH * N_HEADS:]
    return q, gate, z


def compress_rows(rows, cmp_pe, cmp_w1, cmp_w2):
    n = rows.shape[0] // CMP_STRIDE
    ch = rows[:n * CMP_STRIDE].reshape(n, CMP_STRIDE, 2, KV_HEADS, HEAD_DIM)
    blk = jnp.concatenate([ch[:-1], ch[1:]], axis=1)
    blk = blk + jnp.transpose(cmp_pe, (1, 0, 2))[None, :, :, None, :]
    flat = jnp.transpose(blk, (0, 2, 3, 1, 4)).reshape(n - 1, 2, KV_HEADS, CMP_LEN * HEAD_DIM)
    hid = jax.nn.silu(jnp.einsum('nckf,cfe->ncke', flat, cmp_w1))
    return jnp.einsum('ncke,ced->nckd', hid, cmp_w2)


def compress_sample(new_cmp, page_table, pool_cmp, cmp_pe, cmp_w1, cmp_w2):
    past_len = page_table.shape[1] * PAGE_SIZE

    def one(args):
        pages, new_b = args
        past = pool_cmp[pages].reshape(past_len, 2, KV_HEADS, HEAD_DIM)
        return compress_rows(jnp.concatenate([past, new_b], axis=0), cmp_pe, cmp_w1, cmp_w2)

    return lax.map(one, (page_table, new_cmp))


def scores(qg, k, spec):
    return jnp.einsum(spec, qg, k).astype(jnp.float32) * SCALE


def nsa_attend(q, gate, pos, cmp_kv, fetch_sel, n_blocks, win_kv, win_pos):
    nq = q.shape[0]
    n_c = cmp_kv.shape[0]
    qg = q.reshape(nq, KV_HEADS, GROUP, HEAD_DIM)
    c_end = CMP_STRIDE * jnp.arange(n_c) + (CMP_LEN - 1)
    p_cmp = masked_softmax(scores(qg, cmp_kv[:, 0], 'qhgd,nhd->qhgn'),
                           (c_end[None, :] <= pos[:, None])[:, None, None, :])
    o_cmp = jnp.einsum('qhgn,nhd->qhgd', p_cmp.astype(q.dtype), cmp_kv[:, 1])
    ratio = SEL_BLOCK // CMP_STRIDE
    ci = jnp.arange(n_c)[:, None]
    bj = jnp.arange(n_blocks)[None, :]
    member = ((ci >= ratio * bj - (CMP_LEN // CMP_STRIDE - 1)) & (ci <= ratio * bj + ratio - 1)).astype(jnp.float32)
    imp = jnp.einsum('qhn,nj->qhj', p_cmp.sum(axis=2), member)
    valid = bj * SEL_BLOCK <= pos[:, None]
    cur = (pos // SEL_BLOCK)[:, None]
    forced = (bj == 0) | (bj == cur) | (bj == cur - 1)
    score = jnp.where((valid & forced)[:, None], FORCE, jnp.where(valid[:, None], imp, -FORCE))
    top_s, idx = lax.top_k(score, min(SEL_TOPK, n_blocks))
    key_pos = idx[..., None] * SEL_BLOCK + jnp.arange(SEL_BLOCK)
    sel = fetch_sel(key_pos).reshape(nq, KV_HEADS, -1, 2, HEAD_DIM)
    sel_mask = ((top_s > -0.5 * FORCE)[..., None] & (key_pos <= pos[:, None, None, None])).reshape(nq, KV_HEADS, 1, -1)
    p_sel = masked_softmax(scores(qg, sel[..., 0, :], 'qhgd,qhnd->qhgn'), sel_mask)
    o_sel = jnp.einsum('qhgn,qhnd->qhgd', p_sel.astype(q.dtype), sel[..., 1, :])
    d_pos = pos[:, None] - win_pos[None, :]
    w_mask = (win_pos[None, :] >= 0) & (d_pos >= 0) & (d_pos <= WINDOW)
    p_win = masked_softmax(scores(qg, win_kv[:, 0], 'qhgd,nhd->qhgn'), w_mask[:, None, None, :])
    o_win = jnp.einsum('qhgn,nhd->qhgd', p_win.astype(q.dtype), win_kv[:, 1])
    g = gate.reshape(nq, KV_HEADS, GROUP, N_BRANCH)
    o = g[..., 0:1] * o_cmp + g[..., 1:2] * o_sel + g[..., 2:3] * o_win
    return o.reshape(nq, D_ATTN)


def nsa_prompt(q, gate, cmp_tok, sel_kv, win_kv):
    B, S = q.shape[:2]
    n_chunks = S // Q_BLOCK
    n_blocks = -(-S // SEL_BLOCK)
    hidx = jnp.arange(KV_HEADS)[None, :, None, None]
    win_pad = jnp.pad(win_kv, ((0, 0), (WINDOW, 0), (0, 0), (0, 0), (0, 0)))

    def one(bc):
        b, c = bc
        c0 = c * Q_BLOCK
        pos = c0 + jnp.arange(Q_BLOCK)
        q_c = lax.dynamic_slice_in_dim(q[b], c0, Q_BLOCK, axis=0)
        g_c = lax.dynamic_slice_in_dim(gate[b], c0, Q_BLOCK, axis=0)
        sel_b = sel_kv[b]

        def fetch(kp):
            return sel_b[kp, :, hidx, :]

        win_c = lax.dynamic_slice_in_dim(win_pad[b], c0, WINDOW + Q_BLOCK, axis=0)
        win_pos = c0 - WINDOW + jnp.arange(WINDOW + Q_BLOCK)
        return nsa_attend(q_c, g_c, pos, cmp_tok[b], fetch, n_blocks, win_c, win_pos)

    bi = jnp.repeat(jnp.arange(B), n_chunks)
    ci = jnp.tile(jnp.arange(n_chunks), B)
    return lax.map(one, (bi, ci)).reshape(B, S, D_ATTN)


def nsa_sample(q, gate, cmp_tok, new_sel, win_all, page_table, pool_sel):
    T = q.shape[1]
    n_pages = page_table.shape[1]
    past_len = n_pages * PAGE_SIZE
    n_blocks = -(-(past_len + T) // SEL_BLOCK)
    pos = past_len + jnp.arange(T)
    n_w = win_all.shape[1]
    win_pos = past_len + T - n_w + jnp.arange(n_w)
    hidx = jnp.arange(KV_HEADS)[None, :, None, None]

    def one(q_b, g_b, cmp_b, new_b, win_b, pages):
        def fetch(kp):
            in_past = kp < past_len
            page = pages[jnp.clip(kp // PAGE_SIZE, 0, n_pages - 1)]
            old = pool_sel[page, kp % PAGE_SIZE, :, hidx, :]
            new = new_b[jnp.clip(kp - past_len, 0, T - 1), :, hidx, :]
            return jnp.where(in_past[..., None, None], old, new)

        return nsa_attend(q_b, g_b, pos, cmp_b, fetch, n_blocks, win_b, win_pos)

    return jax.vmap(one)(q, gate, cmp_tok, new_sel, win_all, page_table)


def setup_inputs(seed: int = 0) -> dict:
    key = jax.random.key(seed)
    ks = jax.random.split(key, 20)
    n_pages = PAST_LEN // PAGE_SIZE
    n_used = DEC_BATCH * n_pages
    n_pool = n_used + max(1, n_used // 4)
    win_keep = min(WINDOW, PAST_LEN)

    def nrm(k, shape, scale):
        return jax.random.normal(k, shape, jnp.float32) * scale

    page_table = jax.random.permutation(ks[0], n_pool)[:n_used].reshape(DEC_BATCH, n_pages).astype(jnp.int32)
    return {
        'x_prompt': nrm(ks[1], (BATCH, SEQ, D_MODEL), 1.0),
        'x_sample': nrm(ks[2], (DEC_BATCH, DEC_SEQ, D_MODEL), 1.0),
        'cache_conv': nrm(ks[3], (N_A, DEC_BATCH, CONV_WIDTH - 1, D_CONV), 1.0),
        'cache_cmp_kv': nrm(ks[4], (n_pool, PAGE_SIZE, 2, KV_HEADS, HEAD_DIM), 1.0),
        'cache_sel_kv': nrm(ks[5], (n_pool, PAGE_SIZE, 2, KV_HEADS, HEAD_DIM), 1.0),
        'cache_win_kv': nrm(ks[6], (DEC_BATCH, win_keep, 2, KV_HEADS, HEAD_DIM), 1.0),
        'page_table': page_table,
        'norm_a': 1.0 + nrm(ks[7], (N_A, D_MODEL), 0.01),
        'conv_w_in': nrm(ks[8], (N_A, D_MODEL, 4 * D_CONV), D_MODEL ** -0.5),
        'conv_w': nrm(ks[9], (N_A, CONV_WIDTH, D_CONV), CONV_WIDTH ** -0.5),
        'conv_w_out': nrm(ks[10], (N_A, D_CONV, D_MODEL), D_CONV ** -0.5),
        'norm_kv': 1.0 + nrm(ks[11], (D_MODEL,), 0.01),
        'w_kv': nrm(ks[12], (D_MODEL, N_BRANCH * 2 * KV_HEADS * HEAD_DIM), D_MODEL ** -0.5),
        'cmp_pe': nrm(ks[13], (2, CMP_LEN, HEAD_DIM), 0.1),
        'cmp_w1': nrm(ks[14], (2, CMP_LEN * HEAD_DIM, CMP_HIDDEN), (CMP_LEN * HEAD_DIM) ** -0.5),
        'cmp_w2': nrm(ks[15], (2, CMP_HIDDEN, HEAD_DIM), CMP_HIDDEN ** -0.5),
        'norm_b': 1.0 + nrm(ks[16], (N_B, D_MODEL), 0.01),
        'w_qg': nrm(ks[17], (N_B, D_MODEL, 2 * D_ATTN + N_BRANCH * N_HEADS), D_MODEL ** -0.5),
        'w_o': nrm(ks[18], (N_B, D_ATTN, D_MODEL), D_ATTN ** -0.5),
        'norm_f': 1.0 + nrm(ks[19], (D_MODEL,), 0.01),
    }


def reference(x_prompt, x_sample, cache_conv, cache_cmp_kv, cache_sel_kv, cache_win_kv, page_table,
              norm_a, conv_w_in, conv_w, conv_w_out, norm_kv, w_kv, cmp_pe, cmp_w1, cmp_w2,
              norm_b, w_qg, w_o, norm_f):
    h_p, h_s = x_prompt, x_sample
    conv_p, conv_s = [], []
    for layer in range(DEPTH):
        if layer < N_A:
            buf0 = jnp.zeros((h_p.shape[0], CONV_WIDTH - 1, D_CONV), h_p.dtype)
            y_p, st_p = short_conv_layer(rmsnorm(h_p, norm_a[layer]), buf0,
                                         conv_w_in[layer], conv_w[layer], conv_w_out[layer])
            y_s, st_s = short_conv_layer(rmsnorm(h_s, norm_a[layer]), cache_conv[layer],
                                         conv_w_in[layer], conv_w[layer], conv_w_out[layer])
            h_p = h_p + y_p
            h_s = h_s + y_s
            conv_p.append(st_p)
            conv_s.append(st_s)
            if layer == N_A - 1:
                kv_p = shared_kv(h_p, norm_kv, w_kv)
                kv_s = shared_kv(h_s, norm_kv, w_kv)
                cmp_tok_p = lax.map(lambda rows: compress_rows(rows, cmp_pe, cmp_w1, cmp_w2), kv_p[:, :, 0])
                cmp_tok_s = compress_sample(kv_s[:, :, 0], page_table, cache_cmp_kv, cmp_pe, cmp_w1, cmp_w2)
                win_all_s = jnp.concatenate([cache_win_kv, kv_s[:, :, 2]], axis=1)
        else:
            i = layer - N_A
            q_p, g_p, z_p = query_side(rmsnorm(h_p, norm_b[i]), w_qg[i])
            o_p = nsa_prompt(q_p, g_p, cmp_tok_p, kv_p[:, :, 1], kv_p[:, :, 2])
            h_p = h_p + (o_p * jax.nn.silu(z_p)) @ w_o[i]
            q_s, g_s, z_s = query_side(rmsnorm(h_s, norm_b[i]), w_qg[i])
            o_s = nsa_sample(q_s, g_s, cmp_tok_s, kv_s[:, :, 1], win_all_s, page_table, cache_sel_kv)
            h_s = h_s + (o_s * jax.nn.silu(z_s)) @ w_o[i]
    y_prompt = rmsnorm(h_p, norm_f)
    y_sample = rmsnorm(h_s, norm_f)
    new_conv_p = jnp.stack(conv_p)
    new_conv_s = jnp.stack(conv_s)
    keep_p = min(WINDOW, x_prompt.shape[1])
    keep_s = min(WINDOW, win_all_s.shape[1])
    return (y_prompt, y_sample, new_conv_p, new_conv_s, kv_p[:, :, 0], kv_s[:, :, 0], kv_p[:, :, 1], kv_s[:, :, 1], kv_p[:, -keep_p:, 2], win_all_s[:, -keep_s:])
```

```python
import functools

import jax
import jax.numpy as jnp
from jax import lax
from jax.experimental import pallas as pl
from jax.experimental.pallas import tpu as pltpu

F32 = jnp.float32
BF16 = jnp.bfloat16

N_HEADS = 16
HEAD_DIM = 64
KV_HEADS = 4
GROUP = N_HEADS // KV_HEADS
KV_WIDTH = KV_HEADS * HEAD_DIM
ROW_WIDTH = 2 * KV_WIDTH
N_BRANCH = 3
CONV_WIDTH = 3
CMP_STRIDE = 16
CMP_LEN = 2 * CMP_STRIDE
SEL_BLOCK = 64
SEL_TOPK = 16
WINDOW = 512
PAGE_SIZE = 128
EPS = 1e-6
NEG = -1e30
FORCE = 1e9
SCALE = HEAD_DIM ** -0.5

V7X_LANES = 128
V7X_SUBLANES = 8
V7X_VMEM_LIMIT_BYTES = 56 * 1024 * 1024

CONV_CHUNK = 256
TOKEN_TILE_CONV = 512
TOKEN_TILE_PROJ = 256
Q_TILE = 256
CMP_PAGES_PER_STEP = 32
GATE_ROWS = 128

_NT = (((1,), (1,)), ((), ()))
_TN = (((0,), (0,)), ((), ()))


def _params(sem):
    return pltpu.CompilerParams(dimension_semantics=sem, vmem_limit_bytes=V7X_VMEM_LIMIT_BYTES)


def _rmsnorm(x, g):
    return x * lax.rsqrt(jnp.mean(x * x, axis=-1, keepdims=True) + EPS) * g


def _sigmoid(x):
    return 1.0 / (1.0 + jnp.exp(-x))


def _silu(x):
    return x * _sigmoid(x)


def _dot(a, b):
    return jnp.dot(a, b, preferred_element_type=F32)


def _dot_nt(a, b):
    return lax.dot_general(a, b, _NT, preferred_element_type=F32)


def _conv_gate(xn_s, win_ref, cw):
    proj = _dot(xn_s[...], win_ref[0])
    bg, cg, hh, z = (proj[:, k * cw:(k + 1) * cw] for k in range(4))
    return bg, cg * hh, z


def _conv_prompt_kernel(x_ref, g_ref, win_ref, cw_ref, wout_ref, h1_ref, st_ref,
                        xn_s, acc_s, ubuf_s, carry_s, *, tiles_per_seq):
    i = pl.program_id(0)
    j = pl.program_id(1)
    t = x_ref.shape[0]
    cw = cw_ref.shape[1]

    @pl.when(j == 0)
    def _():
        xn_s[...] = _rmsnorm(x_ref[...], g_ref[...]).astype(BF16)
        acc_s[...] = jnp.zeros_like(acc_s)

    bg, u, z = _conv_gate(xn_s, win_ref, cw)
    @pl.when(i % tiles_per_seq == 0)
    def _():
        carry_s[j] = jnp.zeros(carry_s.shape[1:], F32)

    ubuf_s[0:V7X_SUBLANES, :] = carry_s[j]
    ubuf_s[V7X_SUBLANES:V7X_SUBLANES + t, :] = u
    w = cw_ref[...]
    v = (w[0:1] * ubuf_s[V7X_SUBLANES - 2:V7X_SUBLANES - 2 + t, :]
         + w[1:2] * ubuf_s[V7X_SUBLANES - 1:V7X_SUBLANES - 1 + t, :] + w[2:3] * u)
    carry_s[j] = u[t - V7X_SUBLANES:t]
    st_ref[0, j] = ubuf_s[V7X_SUBLANES + t - 2:V7X_SUBLANES + t, :]
    gated = _silu(z) * bg * v
    acc_s[...] += _dot(gated.astype(BF16), wout_ref[...])

    @pl.when(j == pl.num_programs(1) - 1)
    def _():
        h1_ref[...] = x_ref[...] + acc_s[...]


def _conv_step_kernel(x_ref, g_ref, win_ref, cw_ref, wout_ref, p0_ref, p1_ref, h1_ref, u_ref,
                      xn_s, acc_s):
    j = pl.program_id(0)
    cw = cw_ref.shape[1]

    @pl.when(j == 0)
    def _():
        xn_s[...] = _rmsnorm(x_ref[...], g_ref[...]).astype(BF16)
        acc_s[...] = jnp.zeros_like(acc_s)

    bg, u, z = _conv_gate(xn_s, win_ref, cw)
    w = cw_ref[...]
    v = w[0:1] * p0_ref[...] + w[1:2] * p1_ref[...] + w[2:3] * u
    u_ref[...] = u
    gated = _silu(z) * bg * v
    acc_s[...] += _dot(gated.astype(BF16), wout_ref[...])

    @pl.when(j == pl.num_programs(0) - 1)
    def _():
        h1_ref[...] = x_ref[...] + acc_s[...]


def _conv_weights(w_in, w_out):
    d, four_dc = w_in.shape
    dc = four_dc // 4
    nc = dc // CONV_CHUNK
    w = w_in.reshape(d, 4, nc, CONV_CHUNK).transpose(2, 0, 1, 3).reshape(nc, d, 4 * CONV_CHUNK)
    return w.astype(BF16), w_out.astype(BF16)


def _conv_prompt(x, g, win_r, conv_w, wout, seq):
    n, d = x.shape
    nc = win_r.shape[0]
    dc = nc * CONV_CHUNK
    t = min(TOKEN_TILE_CONV, seq)
    assert seq % t == 0 and t % V7X_SUBLANES == 0
    tiles_per_seq = seq // t
    return pl.pallas_call(
        functools.partial(_conv_prompt_kernel, tiles_per_seq=tiles_per_seq),
        grid=(n // t, nc),
        in_specs=[
            pl.BlockSpec((t, d), lambda i, j: (i, 0)),
            pl.BlockSpec((1, d), lambda i, j: (0, 0)),
            pl.BlockSpec((1, d, 4 * CONV_CHUNK), lambda i, j: (j, 0, 0)),
            pl.BlockSpec((CONV_WIDTH, CONV_CHUNK), lambda i, j: (0, j)),
            pl.BlockSpec((CONV_CHUNK, d), lambda i, j: (j, 0)),
        ],
        out_specs=[
            pl.BlockSpec((t, d), lambda i, j: (i, 0)),
            pl.BlockSpec((1, nc, CONV_WIDTH - 1, CONV_CHUNK), lambda i, j: (i // tiles_per_seq, 0, 0, 0)),
        ],
        out_shape=[
            jax.ShapeDtypeStruct((n, d), F32),
            jax.ShapeDtypeStruct((n // seq, nc, CONV_WIDTH - 1, CONV_CHUNK), F32),
        ],
        scratch_shapes=[
            pltpu.VMEM((t, d), BF16),
            pltpu.VMEM((t, d), F32),
            pltpu.VMEM((t + V7X_SUBLANES, CONV_CHUNK), F32),
            pltpu.VMEM((nc, V7X_SUBLANES, CONV_CHUNK), F32),
        ],
        compiler_params=_params(("arbitrary", "arbitrary")),
        name="conv_prompt",
    )(x, g, win_r, conv_w, wout)


def _conv_step(x, g, win_r, conv_w, wout, prev0, prev1):
    n, d = x.shape
    nc = win_r.shape[0]
    dc = nc * CONV_CHUNK
    return pl.pallas_call(
        _conv_step_kernel,
        grid=(nc,),
        in_specs=[
            pl.BlockSpec((n, d), lambda j: (0, 0)),
            pl.BlockSpec((1, d), lambda j: (0, 0)),
            pl.BlockSpec((1, d, 4 * CONV_CHUNK), lambda j: (j, 0, 0)),
            pl.BlockSpec((CONV_WIDTH, CONV_CHUNK), lambda j: (0, j)),
            pl.BlockSpec((CONV_CHUNK, d), lambda j: (j, 0)),
            pl.BlockSpec((n, CONV_CHUNK), lambda j: (0, j)),
            pl.BlockSpec((n, CONV_CHUNK), lambda j: (0, j)),
        ],
        out_specs=[
            pl.BlockSpec((n, d), lambda j: (0, 0)),
            pl.BlockSpec((n, CONV_CHUNK), lambda j: (0, j)),
        ],
        out_shape=[jax.ShapeDtypeStruct((n, d), F32), jax.ShapeDtypeStruct((n, dc), F32)],
        scratch_shapes=[pltpu.VMEM((n, d), BF16), pltpu.VMEM((n, d), F32)],
        compiler_params=_params(("arbitrary",)),
        name="conv_step",
    )(x, g, win_r, conv_w, wout, prev0, prev1)


def _proj_prompt_kernel(h_ref, gkv_ref, gq_ref, wkv_ref, wvt_ref, wqzt_ref,
                        kvc_ref, kvs_ref, kvw_ref, ksel_ref, kwin_ref, vts_ref, vtw_ref,
                        qt_ref, szt_ref, gt_ref, *, tiles_per_seq):
    i = pl.program_id(0)
    t = h_ref.shape[0]
    d_attn = N_HEADS * HEAD_DIM
    nkv = N_BRANCH * ROW_WIDTH
    h = h_ref[...]
    kvn = _rmsnorm(h, gkv_ref[...]).astype(BF16)
    kv = _dot(kvn, wkv_ref[...])
    kvc_ref[...] = kv[:, 0:ROW_WIDTH]
    kvs_ref[...] = kv[:, ROW_WIDTH:2 * ROW_WIDTH]
    kvw_ref[...] = kv[:, 2 * ROW_WIDTH:nkv]
    pos = (i % tiles_per_seq) * t + lax.broadcasted_iota(jnp.int32, (t, V7X_LANES), 0)
    lane = lax.broadcasted_iota(jnp.int32, (t, V7X_LANES), 1)
    onehot = (lane - HEAD_DIM == pos // SEL_BLOCK).astype(F32)
    for k in range(KV_HEADS):
        lo = nkv + k * V7X_LANES
        ksel_ref[0, k] = (kv[:, lo:lo + V7X_LANES] + onehot).astype(BF16)
        lo = nkv + KV_HEADS * V7X_LANES + k * V7X_LANES
        kwin_ref[0, k] = kv[:, lo:lo + V7X_LANES].astype(BF16)
    vt = _dot_nt(wvt_ref[...], kvn)
    for k in range(KV_HEADS):
        vts_ref[0, k] = vt[k * HEAD_DIM:(k + 1) * HEAD_DIM].astype(BF16)
        vtw_ref[0, k] = vt[KV_WIDTH + k * HEAD_DIM:KV_WIDTH + (k + 1) * HEAD_DIM].astype(BF16)
    qn = _rmsnorm(h, gq_ref[...]).astype(BF16)
    qz = _dot_nt(wqzt_ref[...], qn)
    qt_ref[...] = (qz[0:d_attn] * SCALE).astype(BF16)
    szt_ref[...] = _silu(qz[d_attn:2 * d_attn]).astype(BF16)
    gt_ref[...] = _sigmoid(qz[2 * d_attn:2 * d_attn + GATE_ROWS])


def _proj_step_kernel(h_ref, gkv_ref, gq_ref, wkv_ref, wq_ref, wz_ref, wg_ref,
                      kv_ref, q_ref, sz_ref, ge_ref):
    h = h_ref[...]
    kvn = _rmsnorm(h, gkv_ref[...]).astype(BF16)
    kv_ref[...] = _dot(kvn, wkv_ref[...])
    qn = _rmsnorm(h, gq_ref[...]).astype(BF16)
    q_ref[...] = _dot(qn, wq_ref[...]) * SCALE
    sz_ref[...] = _silu(_dot(qn, wz_ref[...]))
    ge_ref[...] = _sigmoid(_dot(qn, wg_ref[...]))


def _proj_weights(w_kv, w_qg):
    d = w_kv.shape[0]
    d_attn = N_HEADS * HEAD_DIM
    n_gate = N_BRANCH * N_HEADS
    wkv4 = w_kv.reshape(d, N_BRANCH, 2, KV_HEADS, HEAD_DIM)
    pad = jnp.zeros((d, KV_HEADS, V7X_LANES - HEAD_DIM), w_kv.dtype)
    ksel = jnp.concatenate([wkv4[:, 1, 0], pad], axis=2).reshape(d, KV_HEADS * V7X_LANES)
    kwin = jnp.concatenate([wkv4[:, 2, 0], pad], axis=2).reshape(d, KV_HEADS * V7X_LANES)
    wkv_all = jnp.concatenate([w_kv, ksel, kwin], axis=1).astype(BF16)
    wvt = jnp.concatenate([wkv4[:, 1, 1].reshape(d, KV_WIDTH), wkv4[:, 2, 1].reshape(d, KV_WIDTH)],
                          axis=1).T.astype(BF16)
    w_q = w_qg[:, :d_attn]
    w_g = w_qg[:, d_attn:d_attn + n_gate]
    w_z = w_qg[:, d_attn + n_gate:]
    gpad = jnp.zeros((d, GATE_ROWS - n_gate), w_qg.dtype)
    wqzt = jnp.concatenate([w_q, w_z, w_g, gpad], axis=1).T.astype(BF16)
    perm = lambda w: w.reshape(d, KV_HEADS, GROUP, HEAD_DIM).transpose(0, 2, 1, 3).reshape(d, d_attn)
    wg_exp = w_g.reshape(d, KV_HEADS, GROUP, N_BRANCH).transpose(0, 3, 2, 1)
    wg_exp = jnp.broadcast_to(wg_exp[..., None], (d, N_BRANCH, GROUP, KV_HEADS, HEAD_DIM))
    wg_exp = wg_exp.reshape(d, N_BRANCH * GROUP * KV_WIDTH)
    return dict(wkv_all=wkv_all, wvt=wvt, wqzt=wqzt, wkv=w_kv.astype(BF16),
                wq_perm=perm(w_q).astype(BF16), wz_perm=perm(w_z).astype(BF16),
                wg_exp=wg_exp.astype(BF16))


def _proj_prompt(h1, gkv, gq, pw, seq):
    n, d = h1.shape
    t = TOKEN_TILE_PROJ
    assert seq % t == 0 and t == Q_TILE and seq // SEL_BLOCK <= V7X_LANES - HEAD_DIM
    d_attn = N_HEADS * HEAD_DIM
    nt = n // t
    full = lambda a: pl.BlockSpec(a.shape, lambda i: (0,) * a.ndim)
    tok = lambda w: pl.BlockSpec((t, w), lambda i: (i, 0))
    ktile = pl.BlockSpec((1, KV_HEADS, t, V7X_LANES), lambda i: (i, 0, 0, 0))
    vtile = pl.BlockSpec((1, KV_HEADS, HEAD_DIM, t), lambda i: (i, 0, 0, 0))
    col = lambda r: pl.BlockSpec((r, t), lambda i: (0, i))
    return pl.pallas_call(
        functools.partial(_proj_prompt_kernel, tiles_per_seq=seq // t),
        grid=(nt,),
        in_specs=[tok(d), full(gkv), full(gq), full(pw["wkv_all"]), full(pw["wvt"]), full(pw["wqzt"])],
        out_specs=[tok(ROW_WIDTH), tok(ROW_WIDTH), tok(ROW_WIDTH), ktile, ktile, vtile, vtile,
                   col(d_attn), col(d_attn), col(GATE_ROWS)],
        out_shape=[
            jax.ShapeDtypeStruct((n, ROW_WIDTH), F32),
            jax.ShapeDtypeStruct((n, ROW_WIDTH), F32),
            jax.ShapeDtypeStruct((n, ROW_WIDTH), F32),
            jax.ShapeDtypeStruct((nt, KV_HEADS, t, V7X_LANES), BF16),
            jax.ShapeDtypeStruct((nt, KV_HEADS, t, V7X_LANES), BF16),
            jax.ShapeDtypeStruct((nt, KV_HEADS, HEAD_DIM, t), BF16),
            jax.ShapeDtypeStruct((nt, KV_HEADS, HEAD_DIM, t), BF16),
            jax.ShapeDtypeStruct((d_attn, n), BF16),
            jax.ShapeDtypeStruct((d_attn, n), BF16),
            jax.ShapeDtypeStruct((GATE_ROWS, n), F32),
        ],
        compiler_params=_params(("arbitrary",)),
        name="proj_prompt",
    )(h1, gkv, gq, pw["wkv_all"], pw["wvt"], pw["wqzt"])


def _proj_step(h1, gkv, gq, pw):
    n, d = h1.shape
    d_attn = N_HEADS * HEAD_DIM
    ins = [h1, gkv, gq, pw["wkv"], pw["wq_perm"], pw["wz_perm"], pw["wg_exp"]]
    outs = [N_BRANCH * ROW_WIDTH, d_attn, d_attn, N_BRANCH * GROUP * KV_WIDTH]
    return pl.pallas_call(
        _proj_step_kernel,
        grid=(1,),
        in_specs=[pl.BlockSpec(a.shape, lambda i: (0,) * a.ndim) for a in ins],
        out_specs=[pl.BlockSpec((n, w), lambda i: (0, 0)) for w in outs],
        out_shape=[jax.ShapeDtypeStruct((n, w), F32) for w in outs],
        compiler_params=_params(("arbitrary",)),
        name="proj_step",
    )(*ins)


CHUNK_FEATS = CMP_STRIDE * ROW_WIDTH
AB_WIDTH = 2 * 2 * 2 * V7X_LANES


def _compress_cols(get_cols, w_ref):
    outs = []
    for c in range(2):
        for kp in range(KV_HEADS // 2):
            base = c * KV_WIDTH + kp * V7X_LANES
            lhs = jnp.concatenate(
                [get_cols(r * ROW_WIDTH + base).astype(BF16) for r in range(CMP_STRIDE)], axis=1)
            outs.append(_dot(lhs, w_ref[c]))
    return jnp.concatenate(outs, axis=1)


def _compress_rows_kernel(x_ref, w_ref, ab_ref):
    ab_ref[0] = _compress_cols(lambda lo: x_ref[0, :, lo:lo + V7X_LANES], w_ref)


def _compress_paged_kernel(pt_ref, pool_ref, w_ref, ab_ref, buf, sem):
    step = pl.program_id(0) * pl.num_programs(1) + pl.program_id(1)
    n_steps = pl.num_programs(0) * pl.num_programs(1)
    pages = buf.shape[1]
    slot = step % 2

    def copies(s, sl):
        return [pltpu.make_async_copy(pool_ref.at[pt_ref[s * pages + p]], buf.at[sl, p], sem.at[sl])
                for p in range(pages)]

    @pl.when(step == 0)
    def _():
        for cp in copies(0, 0):
            cp.start()

    @pl.when(step + 1 < n_steps)
    def _():
        for cp in copies(step + 1, 1 - slot):
            cp.start()

    for cp in copies(step, slot):
        cp.wait()
    m = pages * V7X_SUBLANES
    ab_ref[0] = _compress_cols(
        lambda lo: buf[slot, :, :, lo:lo + V7X_LANES].reshape(m, V7X_LANES), w_ref)


def _compress_weights(cmp_pe, cmp_w1, cmp_w2):
    hid = cmp_w1.shape[2]
    w1r = cmp_w1.reshape(2, 2, CMP_STRIDE, HEAD_DIM, hid)
    eye = jnp.eye(2, dtype=cmp_w1.dtype)
    w = jnp.einsum("cardE,kl->crkdalE", w1r, eye)
    w = w.reshape(2, CMP_STRIDE * 2 * HEAD_DIM, 2 * 2 * hid).astype(BF16)
    pe = cmp_pe.reshape(2, 2, CMP_STRIDE, HEAD_DIM).transpose(1, 2, 0, 3)
    pe = jnp.broadcast_to(pe[:, :, :, None, :], (2, CMP_STRIDE, 2, KV_HEADS, HEAD_DIM))
    pe_rows = jnp.zeros((V7X_SUBLANES, CHUNK_FEATS), F32).at[0:2].set(pe.reshape(2, CHUNK_FEATS))
    eye4 = jnp.eye(KV_HEADS, dtype=cmp_w2.dtype)
    w2k = jnp.einsum("ed,kl->keld", cmp_w2[0], eye4)
    w2k_pad = jnp.concatenate([w2k, jnp.zeros_like(w2k)], axis=3)
    w2k_pad = w2k_pad.reshape(KV_HEADS * hid, KV_HEADS * V7X_LANES).astype(BF16)
    w2k_bd = w2k.reshape(KV_HEADS * hid, KV_WIDTH).astype(BF16)
    w2v_bd = jnp.einsum("ed,kl->keld", cmp_w2[1], eye4).reshape(KV_HEADS * hid, KV_WIDTH).astype(BF16)
    return dict(w1=w, pe_rows=pe_rows, w2k_pad=w2k_pad, w2k_bd=w2k_bd, w2v_bd=w2v_bd,
                w2vt=w2v_bd.T)


def _compress_rows(x, w1):
    ns, nck, feats = x.shape
    return pl.pallas_call(
        _compress_rows_kernel,
        grid=(ns,),
        in_specs=[pl.BlockSpec((1, nck, feats), lambda b: (b, 0, 0)),
                  pl.BlockSpec(w1.shape, lambda b: (0, 0, 0))],
        out_specs=pl.BlockSpec((1, nck, AB_WIDTH), lambda b: (b, 0, 0)),
        out_shape=jax.ShapeDtypeStruct((ns, nck, AB_WIDTH), F32),
        compiler_params=_params(("arbitrary",)),
        name="compress_rows",
    )(x, w1)


def _compress_paged(page_table, pool, w1):
    ns, n_pages = page_table.shape
    chunks_per_page = pool.shape[1]
    pages = min(CMP_PAGES_PER_STEP, n_pages)
    assert n_pages % pages == 0 and chunks_per_page == V7X_SUBLANES
    steps = n_pages // pages
    m = pages * chunks_per_page
    return pl.pallas_call(
        _compress_paged_kernel,
        grid_spec=pltpu.PrefetchScalarGridSpec(
            num_scalar_prefetch=1,
            grid=(ns, steps),
            in_specs=[pl.BlockSpec(memory_space=pl.ANY),
                      pl.BlockSpec(w1.shape, lambda b, s, pt: (0, 0, 0))],
            out_specs=pl.BlockSpec((1, m, AB_WIDTH), lambda b, s, pt: (b, s, 0)),
            scratch_shapes=[pltpu.VMEM((2, pages, chunks_per_page, CHUNK_FEATS), F32),
                            pltpu.SemaphoreType.DMA((2,))],
        ),
        out_shape=jax.ShapeDtypeStruct((ns, n_pages * chunks_per_page, AB_WIDTH), F32),
        compiler_params=_params(("arbitrary", "arbitrary")),
        name="compress_paged",
    )(page_table.reshape(-1), pool, w1)


def _cmp_hidden(ab, bias):
    n = ab.shape[0]
    hid = []
    for grp in range(4):
        base = grp * 2 * V7X_LANES
        a = ab[:, base:base + V7X_LANES] + bias[0:1, base:base + V7X_LANES]
        b = ab[:, base + V7X_LANES:base + 2 * V7X_LANES] + bias[1:2, base + V7X_LANES:base + 2 * V7X_LANES]
        hid.append(_silu(a + pltpu.roll(b, n - 1, 0)))
    return (jnp.concatenate(hid[0:2], axis=1).astype(BF16),
            jnp.concatenate(hid[2:4], axis=1).astype(BF16))


def _cmp_tokens_prompt_kernel(ab_ref, bias_ref, w2k_ref, w2vt_ref, kc_ref, vtc_ref):
    hk, hv = _cmp_hidden(ab_ref[0], bias_ref[0])
    k = _dot(hk, w2k_ref[...]).astype(BF16)
    vt = _dot_nt(w2vt_ref[...], hv).astype(BF16)
    for h in range(KV_HEADS):
        kc_ref[0, h] = k[:, h * V7X_LANES:(h + 1) * V7X_LANES]
        vtc_ref[0, h] = vt[h * HEAD_DIM:(h + 1) * HEAD_DIM]


def _cmp_tokens_prompt(ab, bias, cw):
    ns, nck, _ = ab.shape
    return pl.pallas_call(
        _cmp_tokens_prompt_kernel,
        grid=(ns,),
        in_specs=[pl.BlockSpec((1, nck, AB_WIDTH), lambda b: (b, 0, 0)),
                  pl.BlockSpec(bias.shape, lambda b: (0, 0, 0)),
                  pl.BlockSpec(cw["w2k_pad"].shape, lambda b: (0, 0)),
                  pl.BlockSpec(cw["w2vt"].shape, lambda b: (0, 0))],
        out_specs=[pl.BlockSpec((1, KV_HEADS, nck, V7X_LANES), lambda b: (b, 0, 0, 0)),
                   pl.BlockSpec((1, KV_HEADS, HEAD_DIM, nck), lambda b: (b, 0, 0, 0))],
        out_shape=[jax.ShapeDtypeStruct((ns, KV_HEADS, nck, V7X_LANES), BF16),
                   jax.ShapeDtypeStruct((ns, KV_HEADS, HEAD_DIM, nck), BF16)],
        compiler_params=_params(("arbitrary",)),
        name="cmp_tokens_prompt",
    )(ab, bias, cw["w2k_pad"], cw["w2vt"])


def _topk_pick(score, k, axis):
    n = score.shape[axis]
    idx = lax.broadcasted_iota(jnp.int32, score.shape, axis).astype(F32)
    s = score
    for _ in range(k):
        m = jnp.max(s, axis=axis, keepdims=True)
        first = jnp.min(jnp.where(s == m, idx, float(n)), axis=axis, keepdims=True)
        s = jnp.where(idx == first, -jnp.inf, s)
        yield m, first.astype(jnp.int32)


def _block_scores(imp, blk, pos, n_blocks):
    valid = blk * SEL_BLOCK <= pos
    cur = pos // SEL_BLOCK
    forced = (blk == 0) | (blk == cur) | (blk == cur - 1)
    score = jnp.where(valid & forced, FORCE, jnp.where(valid, imp, -FORCE))
    return jnp.where(blk < n_blocks, score, -jnp.inf)


def _softmax_tile(state, s, vt):
    m, l, acc = state
    m_new = jnp.maximum(m, jnp.max(s, axis=0, keepdims=True))
    alpha = jnp.exp(m - m_new)
    p = jnp.exp(s - m_new)
    l = alpha * l + jnp.sum(p, axis=0, keepdims=True)
    acc = alpha * acc + _dot(vt, p.astype(BF16))
    return m_new, l, acc


def _softmax_finish(state):
    _, l, acc = state
    return acc * (1.0 / jnp.maximum(l, 1e-30))


def _attn_prompt_kernel(qt_ref, szt_ref, gt_ref, h1_ref, ksel_ref, vts_ref, kwin_ref, vtw_ref,
                        kc_ref, vtc_ref, wo_ref, gf_ref, y_ref, osz_s, *, n_cmp, n_blocks):
    c = pl.program_id(1)
    tq = qt_ref.shape[1]
    rows = GROUP * tq
    ncp = kc_ref.shape[2]
    nbp = V7X_LANES - HEAD_DIM
    c0 = c * tq
    pos = c0 + lax.broadcasted_iota(jnp.int32, (1, tq), 1)
    pos_rows = jnp.concatenate([pos] * GROUP, axis=1)

    kj = lax.broadcasted_iota(jnp.int32, (tq, tq), 0)
    qi = lax.broadcasted_iota(jnp.int32, (tq, tq), 1)
    causal = jnp.concatenate([jnp.where(kj <= qi, 0.0, NEG)] * GROUP, axis=1)
    lower = jnp.concatenate([jnp.where(kj >= qi, 0.0, NEG)] * GROUP, axis=1)
    has1 = jnp.where(c >= 1, 0.0, NEG)
    has2 = jnp.where(c >= 2, 0.0, NEG)

    tok = lax.broadcasted_iota(jnp.int32, (ncp, 1), 0)
    cmp_mask = (CMP_STRIDE * tok + (CMP_LEN - 1) <= pos_rows) & (tok < n_cmp)
    blk = lax.broadcasted_iota(jnp.int32, (nbp, 1), 0)
    ci = lax.broadcasted_iota(jnp.int32, (1, ncp), 1)
    ratio = SEL_BLOCK // CMP_STRIDE
    member_t = ((ci >= ratio * blk - (CMP_LEN // CMP_STRIDE - 1)) & (ci <= ratio * blk + ratio - 1)
                & (ci < n_cmp) & (blk < n_blocks)).astype(BF16)
    k_top = min(SEL_TOPK, n_blocks)
    init = (jnp.full((1, rows), NEG, F32), jnp.zeros((1, rows), F32), jnp.zeros((HEAD_DIM, rows), F32))

    def head_body(kvh, carry):
        q_t = jnp.concatenate(
            [qt_ref[pl.ds(pl.multiple_of((kvh * GROUP + g) * HEAD_DIM, HEAD_DIM), HEAD_DIM), :]
             for g in range(GROUP)], axis=1)
        qa0 = jnp.concatenate([q_t, jnp.zeros_like(q_t)], axis=0)
        sc = jnp.where(cmp_mask, _dot(kc_ref[0, kvh], qa0), NEG)
        e = jnp.exp(sc - jnp.max(sc, axis=0, keepdims=True)) * cmp_mask.astype(F32)
        pc = e * (1.0 / jnp.maximum(jnp.sum(e, axis=0, keepdims=True), 1e-30))
        o_cmp = _dot(vtc_ref[0, kvh], pc.astype(BF16))
        psum = pc[:, 0:tq]
        for g in range(1, GROUP):
            psum = psum + pc[:, g * tq:(g + 1) * tq]
        imp = _dot(member_t, psum.astype(BF16))
        score = _block_scores(imp, blk, pos, n_blocks)
        chosen = jnp.zeros((nbp, tq), jnp.int32)
        for top_s, first in _topk_pick(score, k_top, 0):
            chosen = jnp.where((blk == first) & (top_s > -0.5 * FORCE), 1, chosen)
        mask_t = jnp.where(chosen > 0, 0.0, NEG).astype(BF16)
        qa = jnp.concatenate([q_t] + [jnp.concatenate([mask_t] * GROUP, axis=1)], axis=0)
        st = _softmax_tile(init, _dot(ksel_ref[c, kvh], qa) + causal, vts_ref[c, kvh])
        st = lax.fori_loop(
            0, c, lambda kt, s: _softmax_tile(s, _dot(ksel_ref[kt, kvh], qa), vts_ref[kt, kvh]), st)
        o_sel = _softmax_finish(st)
        st = _softmax_tile(init, _dot(kwin_ref[c, kvh], qa) + causal, vtw_ref[c, kvh])
        c1 = jnp.maximum(c - 1, 0)
        st = _softmax_tile(st, _dot(kwin_ref[c1, kvh], qa) + has1, vtw_ref[c1, kvh])
        c2 = jnp.maximum(c - 2, 0)
        st = _softmax_tile(st, _dot(kwin_ref[c2, kvh], qa) + (lower + has2), vtw_ref[c2, kvh])
        o_win = _softmax_finish(st)
        for g in range(GROUP):
            head = kvh * GROUP + g
            gate = lambda br: gt_ref[pl.ds(head * N_BRANCH + br, 1), :]
            sl = slice(g * tq, (g + 1) * tq)
            o = gate(0) * o_cmp[:, sl] + gate(1) * o_sel[:, sl] + gate(2) * o_win[:, sl]
            hrow = pl.ds(pl.multiple_of(head * HEAD_DIM, HEAD_DIM), HEAD_DIM)
            osz_s[hrow, :] = (o * szt_ref[hrow, :].astype(F32)).astype(BF16)
        return carry

    lax.fori_loop(0, KV_HEADS, head_body, 0)
    y = lax.dot_general(osz_s[...], wo_ref[...], _TN, preferred_element_type=F32)
    y_ref[...] = _rmsnorm(h1_ref[...] + y, gf_ref[...])


def _attn_prompt(qt, szt, gt, h1, ksel, vts, kwin, vtw, kc, vtc, wo, gf, batch, seq):
    n, d = h1.shape
    tq = Q_TILE
    nch = seq // tq
    d_attn = N_HEADS * HEAD_DIM
    n_cmp = seq // CMP_STRIDE - 1
    n_blocks = -(-seq // SEL_BLOCK)
    col = lambda r: pl.BlockSpec((r, tq), lambda b, c: (0, b * nch + c))
    seq_k = pl.BlockSpec((nch, KV_HEADS, tq, V7X_LANES), lambda b, c: (b, 0, 0, 0))
    seq_v = pl.BlockSpec((nch, KV_HEADS, HEAD_DIM, tq), lambda b, c: (b, 0, 0, 0))
    ncp = kc.shape[2]
    return pl.pallas_call(
        functools.partial(_attn_prompt_kernel, n_cmp=n_cmp, n_blocks=n_blocks),
        grid=(batch, nch),
        in_specs=[col(d_attn), col(d_attn), col(GATE_ROWS),
                  pl.BlockSpec((tq, d), lambda b, c: (b * nch + c, 0)),
                  seq_k, seq_v, seq_k, seq_v,
                  pl.BlockSpec((1, KV_HEADS, ncp, V7X_LANES), lambda b, c: (b, 0, 0, 0)),
                  pl.BlockSpec((1, KV_HEADS, HEAD_DIM, ncp), lambda b, c: (b, 0, 0, 0)),
                  pl.BlockSpec(wo.shape, lambda b, c: (0, 0)),
                  pl.BlockSpec(gf.shape, lambda b, c: (0, 0))],
        out_specs=pl.BlockSpec((tq, d), lambda b, c: (b * nch + c, 0)),
        out_shape=jax.ShapeDtypeStruct((n, d), F32),
        scratch_shapes=[pltpu.VMEM((d_attn, tq), BF16)],
        compiler_params=_params(("arbitrary", "arbitrary")),
        name="attn_prompt",
    )(qt, szt, gt, h1, ksel, vts, kwin, vtw, kc, vtc, wo, gf)


HEAD_ROWS = KV_HEADS * V7X_SUBLANES


def _query_rows(q):
    r = lax.broadcasted_iota(jnp.int32, (HEAD_ROWS, KV_WIDTH), 0)
    lane = lax.broadcasted_iota(jnp.int32, (HEAD_ROWS, KV_WIDTH), 1)
    val = jnp.zeros((HEAD_ROWS, KV_WIDTH), F32)
    for g in range(GROUP):
        val = jnp.where(r % V7X_SUBLANES == g, q[:, g * KV_WIDTH:(g + 1) * KV_WIDTH], val)
    return jnp.where(r // V7X_SUBLANES == lane // HEAD_DIM, val, 0.0)


def _own_head_rows(o):
    lane = lax.broadcasted_iota(jnp.int32, (V7X_SUBLANES, KV_WIDTH), 1)
    out = jnp.zeros((V7X_SUBLANES, KV_WIDTH), F32)
    for k in range(KV_HEADS):
        out = jnp.where(lane // HEAD_DIM == k, o[k * V7X_SUBLANES:(k + 1) * V7X_SUBLANES], out)
    return out


def _attn_step_cmp_kernel(ab_ref, bias_ref, w2k_ref, w2v_ref, q_ref, og_ref, idx_ref,
                          *, n_cmp, n_blocks, pos):
    nck = ab_ref.shape[1]
    nbp = idx_ref.shape[2] * (-(-n_blocks // idx_ref.shape[2]))
    hk, hv = _cmp_hidden(ab_ref[0], bias_ref[0])
    k = _dot(hk, w2k_ref[...]).astype(BF16)
    v = _dot(hv, w2v_ref[...]).astype(BF16)
    qb = _query_rows(q_ref[0])
    tok = lax.broadcasted_iota(jnp.int32, (1, nck), 1)
    mask = (CMP_STRIDE * tok + (CMP_LEN - 1) <= pos) & (tok < n_cmp)
    s = jnp.where(mask, _dot_nt(qb.astype(BF16), k), NEG)
    e = jnp.exp(s - jnp.max(s, axis=1, keepdims=True)) * mask.astype(F32)
    p = e * (1.0 / jnp.maximum(jnp.sum(e, axis=1, keepdims=True), 1e-30))
    r = lax.broadcasted_iota(jnp.int32, (HEAD_ROWS, 1), 0)
    p = jnp.where(r % V7X_SUBLANES < GROUP, p, 0.0)
    og_ref[0] = _own_head_rows(_dot(p.astype(BF16), v))
    r8 = lax.broadcasted_iota(jnp.int32, (V7X_SUBLANES, 1), 0)
    psum = jnp.zeros((V7X_SUBLANES, nck), F32)
    for h in range(KV_HEADS):
        ph = jnp.sum(p[h * V7X_SUBLANES:(h + 1) * V7X_SUBLANES], axis=0, keepdims=True)
        psum = jnp.where(r8 == h, ph, psum)
    ci = lax.broadcasted_iota(jnp.int32, (nck, 1), 0)
    blk = lax.broadcasted_iota(jnp.int32, (1, nbp), 1)
    ratio = SEL_BLOCK // CMP_STRIDE
    member = ((ci >= ratio * blk - (CMP_LEN // CMP_STRIDE - 1)) & (ci <= ratio * blk + ratio - 1)
              & (ci < n_cmp) & (blk < n_blocks)).astype(BF16)
    imp = _dot(psum.astype(BF16), member)
    score = _block_scores(imp, blk, pos, n_blocks)
    slot = lax.broadcasted_iota(jnp.int32, idx_ref.shape[1:], 1)
    picked = jnp.full(idx_ref.shape[1:], -1, jnp.int32)
    for j, (top_s, first) in enumerate(_topk_pick(score, min(SEL_TOPK, n_blocks), 1)):
        picked = jnp.where(slot == j, jnp.where(top_s > -0.5 * FORCE, first, -1), picked)
    idx_ref[0] = picked


def _attn_step_cmp(ab, bias, cw, q3, n_cmp, n_blocks, pos):
    ns, nck, _ = ab.shape
    return pl.pallas_call(
        functools.partial(_attn_step_cmp_kernel, n_cmp=n_cmp, n_blocks=n_blocks, pos=pos),
        grid=(ns,),
        in_specs=[pl.BlockSpec((1, nck, AB_WIDTH), lambda b: (b, 0, 0)),
                  pl.BlockSpec(bias.shape, lambda b: (0, 0, 0)),
                  pl.BlockSpec(cw["w2k_bd"].shape, lambda b: (0, 0)),
                  pl.BlockSpec(cw["w2v_bd"].shape, lambda b: (0, 0)),
                  pl.BlockSpec((1, 1, q3.shape[2]), lambda b: (b, 0, 0))],
        out_specs=[pl.BlockSpec((1, V7X_SUBLANES, KV_WIDTH), lambda b: (b, 0, 0)),
                   pl.BlockSpec((1, V7X_SUBLANES, V7X_LANES), lambda b: (b, 0, 0))],
        out_shape=[jax.ShapeDtypeStruct((ns, V7X_SUBLANES, KV_WIDTH), F32),
                   jax.ShapeDtypeStruct((ns, V7X_SUBLANES, V7X_LANES), jnp.int32)],
        compiler_params=_params(("arbitrary",)),
        name="attn_step_cmp",
    )(ab, bias, cw["w2k_bd"], cw["w2v_bd"], q3)


def _attn_step_kernel(idx_ref, pt_ref, pool_ref, q_ref, kvn_ref, win_ref, ogc_ref, ge_ref, sz_ref,
                      osz_ref, buf, sem, *, n_past_blocks, pos, win_pos0):
    b = pl.program_id(0)
    n_slots = buf.shape[0]
    per_head = n_slots // KV_HEADS
    pages_per_seq = pt_ref.shape[0] // pl.num_programs(0)
    halves = PAGE_SIZE // SEL_BLOCK

    def copy(s):
        j = jnp.clip(idx_ref[b * n_slots + s], 0, n_past_blocks - 1)
        page = pt_ref[b * pages_per_seq + j // halves]
        return pltpu.make_async_copy(pool_ref.at[page * halves + j % halves], buf.at[s], sem.at[0])

    for s in range(n_slots):
        copy(s).start()

    qb = _query_rows(q_ref[0])
    qb16 = qb.astype(BF16)
    r = lax.broadcasted_iota(jnp.int32, (HEAD_ROWS, 1), 0)
    new_row = kvn_ref[0]

    def with_new_token(s, v16, k_new, v_new, new_ok):
        s_new = jnp.where(new_ok, jnp.sum(qb * k_new, axis=1, keepdims=True), NEG)
        m = jnp.maximum(jnp.max(s, axis=1, keepdims=True), s_new)
        e = jnp.exp(s - m)
        e_new = jnp.where(new_ok, jnp.exp(s_new - m), 0.0)
        den = jnp.maximum(jnp.sum(e, axis=1, keepdims=True) + e_new, 1e-30)
        return (_dot(e.astype(BF16), v16) + e_new * v_new) * (1.0 / den)

    win = win_ref[0]
    n_win = win.shape[0]
    wpos = win_pos0 + lax.broadcasted_iota(jnp.int32, (1, n_win), 1)
    wmask = (wpos >= 0) & (pos - wpos >= 0) & (pos - wpos <= WINDOW)
    sw = jnp.where(wmask, _dot_nt(qb16, win[:, 0:KV_WIDTH].astype(BF16)), NEG)
    lo = 2 * ROW_WIDTH
    o_win = _own_head_rows(with_new_token(
        sw, win[:, KV_WIDTH:ROW_WIDTH].astype(BF16),
        new_row[:, lo:lo + KV_WIDTH], new_row[:, lo + KV_WIDTH:lo + ROW_WIDTH], True))

    for s in range(n_slots):
        copy(s).wait()

    lane_slot = lax.broadcasted_iota(jnp.int32, (1, per_head * SEL_BLOCK), 1) // SEL_BLOCK
    o_sel = jnp.zeros((HEAD_ROWS, KV_WIDTH), F32)
    for h in range(KV_HEADS):
        rows = buf[h * per_head:(h + 1) * per_head].reshape(per_head * SEL_BLOCK, ROW_WIDTH)
        ok = jnp.zeros((1, per_head * SEL_BLOCK), jnp.int32)
        has_new = jnp.int32(0)
        for j in range(per_head):
            blk = idx_ref[b * n_slots + h * per_head + j]
            past = ((blk >= 0) & (blk < n_past_blocks)).astype(jnp.int32)
            ok = jnp.where(lane_slot == j, past, ok)
            has_new = has_new | ((blk >= n_past_blocks) & (blk * SEL_BLOCK <= pos)).astype(jnp.int32)
        s = jnp.where(ok > 0, _dot_nt(qb16, rows[:, 0:KV_WIDTH].astype(BF16)), NEG)
        o = with_new_token(s, rows[:, KV_WIDTH:ROW_WIDTH].astype(BF16),
                           new_row[:, ROW_WIDTH:ROW_WIDTH + KV_WIDTH],
                           new_row[:, ROW_WIDTH + KV_WIDTH:2 * ROW_WIDTH], has_new > 0)
        o_sel = jnp.where(r // V7X_SUBLANES == h, o, o_sel)
    o_sel = _own_head_rows(o_sel)

    ge = ge_ref[0]
    o = (ge[0] * ogc_ref[0][0:GROUP] + ge[1] * o_sel[0:GROUP] + ge[2] * o_win[0:GROUP])
    osz_ref[0] = o * sz_ref[0]


def _attn_step(idx, page_table, pool_rows, q3, kvn3, win, ogc, ge4, sz3, n_past_blocks, pos, win_pos0):
    ns = q3.shape[0]
    n_slots = KV_HEADS * min(SEL_TOPK, n_past_blocks + 1)
    blk3 = lambda a: pl.BlockSpec((1,) + a.shape[1:], lambda b, i, p: (b,) + (0,) * (a.ndim - 1))
    return pl.pallas_call(
        functools.partial(_attn_step_kernel, n_past_blocks=n_past_blocks, pos=pos, win_pos0=win_pos0),
        grid_spec=pltpu.PrefetchScalarGridSpec(
            num_scalar_prefetch=2,
            grid=(ns,),
            in_specs=[pl.BlockSpec(memory_space=pl.ANY), blk3(q3), blk3(kvn3), blk3(win), blk3(ogc),
                      blk3(ge4), blk3(sz3)],
            out_specs=pl.BlockSpec((1, GROUP, KV_WIDTH), lambda b, i, p: (b, 0, 0)),
            scratch_shapes=[pltpu.VMEM((n_slots, SEL_BLOCK, ROW_WIDTH), F32),
                            pltpu.SemaphoreType.DMA((1,))],
        ),
        out_shape=jax.ShapeDtypeStruct((ns, GROUP, KV_WIDTH), F32),
        compiler_params=_params(("arbitrary",)),
        name="attn_step",
    )(idx, page_table.reshape(-1), pool_rows, q3, kvn3, win, ogc, ge4, sz3)


def _out_step_kernel(osz_ref, h1_ref, wo_ref, gf_ref, y_ref):
    y = _dot(osz_ref[...].astype(BF16), wo_ref[...])
    y_ref[...] = _rmsnorm(h1_ref[...] + y, gf_ref[...])


def _out_step(osz, h1, wo_perm, gf):
    ins = [osz, h1, wo_perm, gf]
    return pl.pallas_call(
        _out_step_kernel,
        grid=(1,),
        in_specs=[pl.BlockSpec(a.shape, lambda i: (0, 0)) for a in ins],
        out_specs=pl.BlockSpec(h1.shape, lambda i: (0, 0)),
        out_shape=jax.ShapeDtypeStruct(h1.shape, F32),
        compiler_params=_params(("arbitrary",)),
        name="out_step",
    )(*ins)


def kernel(x_prompt, x_sample, cache_conv, cache_cmp_kv, cache_sel_kv, cache_win_kv, page_table,
           norm_a, conv_w_in, conv_w, conv_w_out, norm_kv, w_kv, cmp_pe, cmp_w1, cmp_w2,
           norm_b, w_qg, w_o, norm_f):
    batch, seq, d = x_prompt.shape
    dec, dec_seq, _ = x_sample.shape
    assert norm_a.shape[0] == 1 and norm_b.shape[0] == 1 and dec_seq == 1
    assert cache_conv.shape[2] == CONV_WIDTH - 1 and cache_cmp_kv.shape[1] == PAGE_SIZE
    n_pool = cache_cmp_kv.shape[0]
    n_pages = page_table.shape[1]
    past_len = n_pages * PAGE_SIZE
    dc = conv_w.shape[2]

    win_r, wout = _conv_weights(conv_w_in[0], conv_w_out[0])
    pw = _proj_weights(w_kv, w_qg[0])
    cw = _compress_weights(cmp_pe, cmp_w1, cmp_w2)
    g_a, g_kv, g_b, g_f = norm_a[0][None], norm_kv[None], norm_b[0][None], norm_f[None]
    wo16 = w_o[0].astype(BF16)
    wo_perm = wo16.reshape(KV_HEADS, GROUP, HEAD_DIM, d).transpose(1, 0, 2, 3).reshape(N_HEADS * HEAD_DIM, d)
    bias = _compress_rows(cw["pe_rows"][None], cw["w1"])

    n = batch * seq
    h1_p, conv_p = _conv_prompt(x_prompt.reshape(n, d), g_a, win_r, conv_w[0], wout, seq)
    conv_p = conv_p.transpose(0, 2, 1, 3).reshape(batch, CONV_WIDTH - 1, dc)
    (kvc_p, kvs_p, kvw_p, ksel, kwin, vts, vtw, qt, szt, gt) = _proj_prompt(h1_p, g_kv, g_b, pw, seq)
    ab_p = _compress_rows(kvc_p.reshape(batch, seq // CMP_STRIDE, CHUNK_FEATS), cw["w1"])
    kc, vtc = _cmp_tokens_prompt(ab_p, bias, cw)
    y_p = _attn_prompt(qt, szt, gt, h1_p, ksel, vts, kwin, vtw, kc, vtc, wo16, g_f, batch, seq)

    prev = cache_conv[0]
    h1_s, u_s = _conv_step(x_sample.reshape(dec, d), g_a, win_r, conv_w[0], wout, prev[:, 0], prev[:, 1])
    kv_s, q_s, sz_s, ge_s = _proj_step(h1_s, g_kv, g_b, pw)
    ab_s = _compress_paged(page_table, cache_cmp_kv.reshape(n_pool, PAGE_SIZE // CMP_STRIDE, CHUNK_FEATS),
                           cw["w1"])
    n_cmp_s = (past_len + dec_seq) // CMP_STRIDE - 1
    assert n_cmp_s + 1 == ab_s.shape[1]
    n_blocks_s = -(-(past_len + dec_seq) // SEL_BLOCK)
    q3 = q_s[:, None, :]
    ogc, idx = _attn_step_cmp(ab_s, bias, cw, q3, n_cmp_s, n_blocks_s, past_len)
    k_sel = min(SEL_TOPK, n_blocks_s)
    idx = idx[:, :KV_HEADS, :k_sel].reshape(-1)
    n_win = cache_win_kv.shape[1]
    osz = _attn_step(idx, page_table,
                     cache_sel_kv.reshape(n_pool * (PAGE_SIZE // SEL_BLOCK), SEL_BLOCK, ROW_WIDTH),
                     q3, kv_s[:, None, :], cache_win_kv.reshape(dec, n_win, ROW_WIDTH), ogc,
                     ge_s.reshape(dec, N_BRANCH, GROUP, KV_WIDTH), sz_s.reshape(dec, GROUP, KV_WIDTH),
                     past_len // SEL_BLOCK, past_len, past_len - n_win)
    y_s = _out_step(osz.reshape(dec, N_HEADS * HEAD_DIM), h1_s, wo_perm, g_f)

    kv5 = lambda a, lead: a.reshape(lead + (2, KV_HEADS, HEAD_DIM))
    keep_p = min(WINDOW, seq)
    win_kv_p = kv5(kvw_p, (batch, seq))[:, seq - keep_p:]
    kvs3 = kv_s.reshape(dec, N_BRANCH, ROW_WIDTH)
    win_all = jnp.concatenate([cache_win_kv, kv5(kvs3[:, 2], (dec, 1))], axis=1)
    keep_s = min(WINDOW, win_all.shape[1])
    conv_s = jnp.stack([prev[:, 1], u_s], axis=1)[None]
    return (y_p.reshape(batch, seq, d), y_s.reshape(dec, 1, d), conv_p[None], conv_s,
            kv5(kvc_p, (batch, seq)), kv5(kvs3[:, 0], (dec, 1)),
            kv5(kvs_p, (batch, seq)), kv5(kvs3[:, 1], (dec, 1)),
            win_kv_p, win_all[:, win_all.shape[1] - keep_s:])
```

```python
import functools
import math

import jax
import jax.numpy as jnp
from jax import lax
from jax.experimental import pallas as pl
from jax.experimental.pallas import tpu as pltpu

F32 = jnp.float32
BF16 = jnp.bfloat16

N_HEADS = 16
HEAD_DIM = 64
KV_HEADS = 4
GROUP = N_HEADS // KV_HEADS
KV_WIDTH = KV_HEADS * HEAD_DIM
ROW_WIDTH = 2 * KV_WIDTH
N_BRANCH = 3
CONV_WIDTH = 3
CMP_STRIDE = 16
CMP_LEN = 2 * CMP_STRIDE
SEL_BLOCK = 64
SEL_TOPK = 16
WINDOW = 512
PAGE_SIZE = 128
EPS = 1e-6
NEG = -1e30
FORCE = 1e9
SCALE = HEAD_DIM ** -0.5
LOG2E = math.log2(math.e)

V7X_LANES = 128
V7X_SUBLANES = 8
V7X_BF16_ROWS = 16
V7X_VMEM_LIMIT_BYTES = 56 * 1024 * 1024

CONV_CHUNK = 256
TOKEN_TILE_CONV = 1024
TOKEN_TILE_PROJ = 256
Q_TILE = 256
CMP_PAGES_PER_STEP = 32
GATE_ROWS = 128
V_ROWS = HEAD_DIM + V7X_BF16_ROWS

_NT = (((1,), (1,)), ((), ()))
_TN = (((0,), (0,)), ((), ()))


def _params(sem):
    return pltpu.CompilerParams(dimension_semantics=sem, vmem_limit_bytes=V7X_VMEM_LIMIT_BYTES)


def _rmsnorm(x, g):
    return x * lax.rsqrt(jnp.mean(x * x, axis=-1, keepdims=True) + EPS) * g


def _sigmoid(x):
    return 1.0 / (1.0 + jnp.exp(-x))


def _silu(x):
    return x * _sigmoid(x)


def _dot(a, b):
    return jnp.dot(a, b, preferred_element_type=F32)


def _dot_nt(a, b):
    return lax.dot_general(a, b, _NT, preferred_element_type=F32)


def _full(a):
    return pl.BlockSpec(a.shape, lambda *_: (0,) * a.ndim)


def _conv_gate(xn_s, win_ref, cw):
    proj = _dot(xn_s[...], win_ref[0])
    bg, cg, hh, z = (proj[:, k * cw:(k + 1) * cw] for k in range(4))
    return bg, cg * hh, z


def _conv_prompt_kernel(x_ref, g_ref, win_ref, cw_ref, wout_ref, h1_ref, st_ref,
                        xn_s, acc_s, ubuf_s, carry_s, *, tiles_per_seq):
    i = pl.program_id(0)
    j = pl.program_id(1)
    t = x_ref.shape[0]
    cw = cw_ref.shape[1]

    @pl.when(j == 0)
    def _():
        xn_s[...] = _rmsnorm(x_ref[...], g_ref[...]).astype(BF16)
        acc_s[...] = jnp.zeros_like(acc_s)

    @pl.when(i % tiles_per_seq == 0)
    def _():
        carry_s[j] = jnp.zeros(carry_s.shape[1:], F32)

    bg, u, z = _conv_gate(xn_s, win_ref, cw)
    ubuf_s[0:V7X_SUBLANES, :] = carry_s[j]
    ubuf_s[V7X_SUBLANES:V7X_SUBLANES + t, :] = u
    w = cw_ref[...]
    v = (w[0:1] * ubuf_s[V7X_SUBLANES - 2:V7X_SUBLANES - 2 + t, :]
         + w[1:2] * ubuf_s[V7X_SUBLANES - 1:V7X_SUBLANES - 1 + t, :] + w[2:3] * u)
    carry_s[j] = u[t - V7X_SUBLANES:t]
    st_ref[0, j] = ubuf_s[V7X_SUBLANES + t - 2:V7X_SUBLANES + t, :]
    gated = _silu(z) * bg * v
    acc_s[...] += _dot(gated.astype(BF16), wout_ref[...])

    @pl.when(j == pl.num_programs(1) - 1)
    def _():
        h1_ref[...] = x_ref[...] + acc_s[...]


def _conv_step_kernel(x_ref, g_ref, win_ref, cw_ref, wout_ref, p0_ref, p1_ref, h1_ref, u_ref,
                      xn_s, acc_s):
    j = pl.program_id(0)
    cw = cw_ref.shape[1]

    @pl.when(j == 0)
    def _():
        xn_s[...] = _rmsnorm(x_ref[...], g_ref[...]).astype(BF16)
        acc_s[...] = jnp.zeros_like(acc_s)

    bg, u, z = _conv_gate(xn_s, win_ref, cw)
    w = cw_ref[...]
    v = w[0:1] * p0_ref[...] + w[1:2] * p1_ref[...] + w[2:3] * u
    u_ref[...] = u
    gated = _silu(z) * bg * v
    acc_s[...] += _dot(gated.astype(BF16), wout_ref[...])

    @pl.when(j == pl.num_programs(0) - 1)
    def _():
        h1_ref[...] = x_ref[...] + acc_s[...]


def _conv_weights(w_in, w_out):
    d, four_dc = w_in.shape
    dc = four_dc // 4
    nc = dc // CONV_CHUNK
    w = w_in.reshape(d, 4, nc, CONV_CHUNK).transpose(2, 0, 1, 3).reshape(nc, d, 4 * CONV_CHUNK)
    return w.astype(BF16), w_out.astype(BF16)


def _conv_prompt(x, g, win_r, conv_w, wout, seq):
    n, d = x.shape
    nc = win_r.shape[0]
    t = min(TOKEN_TILE_CONV, seq)
    assert seq % t == 0 and t % V7X_SUBLANES == 0
    tiles_per_seq = seq // t
    return pl.pallas_call(
        functools.partial(_conv_prompt_kernel, tiles_per_seq=tiles_per_seq),
        grid=(n // t, nc),
        in_specs=[
            pl.BlockSpec((t, d), lambda i, j: (i, 0)),
            pl.BlockSpec((1, d), lambda i, j: (0, 0)),
            pl.BlockSpec((1, d, 4 * CONV_CHUNK), lambda i, j: (j, 0, 0)),
            pl.BlockSpec((CONV_WIDTH, CONV_CHUNK), lambda i, j: (0, j)),
            pl.BlockSpec((CONV_CHUNK, d), lambda i, j: (j, 0)),
        ],
        out_specs=[
            pl.BlockSpec((t, d), lambda i, j: (i, 0)),
            pl.BlockSpec((1, nc, CONV_WIDTH - 1, CONV_CHUNK), lambda i, j: (i // tiles_per_seq, 0, 0, 0)),
        ],
        out_shape=[
            jax.ShapeDtypeStruct((n, d), F32),
            jax.ShapeDtypeStruct((n // seq, nc, CONV_WIDTH - 1, CONV_CHUNK), F32),
        ],
        scratch_shapes=[
            pltpu.VMEM((t, d), BF16),
            pltpu.VMEM((t, d), F32),
            pltpu.VMEM((t + V7X_SUBLANES, CONV_CHUNK), F32),
            pltpu.VMEM((nc, V7X_SUBLANES, CONV_CHUNK), F32),
        ],
        compiler_params=_params(("arbitrary", "arbitrary")),
        name="conv_prompt",
    )(x, g, win_r, conv_w, wout)


def _conv_step(x, g, win_r, conv_w, wout, prev0, prev1):
    n, d = x.shape
    nc = win_r.shape[0]
    dc = nc * CONV_CHUNK
    return pl.pallas_call(
        _conv_step_kernel,
        grid=(nc,),
        in_specs=[
            pl.BlockSpec((n, d), lambda j: (0, 0)),
            pl.BlockSpec((1, d), lambda j: (0, 0)),
            pl.BlockSpec((1, d, 4 * CONV_CHUNK), lambda j: (j, 0, 0)),
            pl.BlockSpec((CONV_WIDTH, CONV_CHUNK), lambda j: (0, j)),
            pl.BlockSpec((CONV_CHUNK, d), lambda j: (j, 0)),
            pl.BlockSpec((n, CONV_CHUNK), lambda j: (0, j)),
            pl.BlockSpec((n, CONV_CHUNK), lambda j: (0, j)),
        ],
        out_specs=[
            pl.BlockSpec((n, d), lambda j: (0, 0)),
            pl.BlockSpec((n, CONV_CHUNK), lambda j: (0, j)),
        ],
        out_shape=[jax.ShapeDtypeStruct((n, d), F32), jax.ShapeDtypeStruct((n, dc), F32)],
        scratch_shapes=[pltpu.VMEM((n, d), BF16), pltpu.VMEM((n, d), F32)],
        compiler_params=_params(("arbitrary",)),
        name="conv_step",
    )(x, g, win_r, conv_w, wout, prev0, prev1)


def _proj_prompt_kernel(h_ref, gkv_ref, gq_ref, wrow_ref, wkvt_ref, wqzt_ref,
                        kvtc_ref, kvts_ref, kvtw_ref, kcmp_ref, ksel_ref, kwin_ref, vts_ref, vtw_ref,
                        qt_ref, szt_ref, gt_ref, *, tiles_per_seq):
    i = pl.program_id(0)
    t = h_ref.shape[0]
    d_attn = N_HEADS * HEAD_DIM
    h = h_ref[...]
    kvn = _rmsnorm(h, gkv_ref[...]).astype(BF16)
    kvt = _dot_nt(wkvt_ref[...], kvn)
    kvtc_ref[0] = kvt[0:ROW_WIDTH]
    kvts_ref[0] = kvt[ROW_WIDTH:2 * ROW_WIDTH]
    kvtw_ref[0] = kvt[2 * ROW_WIDTH:3 * ROW_WIDTH]
    ones = jnp.ones((V_ROWS - HEAD_DIM, t), BF16)
    for k in range(KV_HEADS):
        lo = ROW_WIDTH + KV_WIDTH + k * HEAD_DIM
        vts_ref[0, k, 0:HEAD_DIM] = kvt[lo:lo + HEAD_DIM].astype(BF16)
        vts_ref[0, k, HEAD_DIM:V_ROWS] = ones
        lo = 2 * ROW_WIDTH + KV_WIDTH + k * HEAD_DIM
        vtw_ref[0, k, 0:HEAD_DIM] = kvt[lo:lo + HEAD_DIM].astype(BF16)
        vtw_ref[0, k, HEAD_DIM:V_ROWS] = ones
    row = _dot(kvn, wrow_ref[...])
    for grp in range(ROW_WIDTH // V7X_LANES):
        kcmp_ref[grp] = row[:, grp * V7X_LANES:(grp + 1) * V7X_LANES]
    pos = (i % tiles_per_seq) * t + lax.broadcasted_iota(jnp.int32, (t, V7X_LANES), 0)
    lane = lax.broadcasted_iota(jnp.int32, (t, V7X_LANES), 1)
    onehot = (lane - HEAD_DIM == pos // SEL_BLOCK).astype(F32)
    for k in range(KV_HEADS):
        lo = ROW_WIDTH + k * V7X_LANES
        ksel_ref[0, k] = (row[:, lo:lo + V7X_LANES] + onehot).astype(BF16)
        lo = ROW_WIDTH + (KV_HEADS + k) * V7X_LANES
        kwin_ref[0, k] = row[:, lo:lo + V7X_LANES].astype(BF16)
    qn = _rmsnorm(h, gq_ref[...]).astype(BF16)
    qz = _dot_nt(wqzt_ref[...], qn)
    qt_ref[...] = (qz[0:d_attn] * (SCALE * LOG2E)).astype(BF16)
    szt_ref[...] = _silu(qz[d_attn:2 * d_attn]).astype(BF16)
    gt_ref[...] = _sigmoid(qz[2 * d_attn:2 * d_attn + GATE_ROWS])


def _proj_step_kernel(h_ref, gkv_ref, gq_ref, wkvt_ref, wqzt_ref, kv_ref, q_ref, sz_ref, g_ref):
    d_attn = N_HEADS * HEAD_DIM
    h = h_ref[...]
    kvn = _rmsnorm(h, gkv_ref[...]).astype(BF16)
    kv_ref[...] = _dot_nt(kvn, wkvt_ref[...])
    qn = _rmsnorm(h, gq_ref[...]).astype(BF16)
    qz = _dot_nt(qn, wqzt_ref[...])
    q_ref[...] = qz[:, 0:d_attn] * SCALE
    sz_ref[...] = _silu(qz[:, d_attn:2 * d_attn])
    g_ref[...] = _sigmoid(qz[:, 2 * d_attn:2 * d_attn + GATE_ROWS])


def _proj_weights(w_kv, w_qg):
    d = w_kv.shape[0]
    d_attn = N_HEADS * HEAD_DIM
    n_gate = N_BRANCH * N_HEADS
    wkv4 = w_kv.reshape(d, N_BRANCH, 2, KV_HEADS, HEAD_DIM)
    pad = jnp.zeros((d, KV_HEADS, V7X_LANES - HEAD_DIM), w_kv.dtype)
    ksel = jnp.concatenate([wkv4[:, 1, 0], pad], axis=2).reshape(d, KV_HEADS * V7X_LANES)
    kwin = jnp.concatenate([wkv4[:, 2, 0], pad], axis=2).reshape(d, KV_HEADS * V7X_LANES)
    wrow = jnp.concatenate([w_kv[:, 0:ROW_WIDTH], ksel, kwin], axis=1).astype(BF16)
    w_q = w_qg[:, :d_attn]
    w_g = w_qg[:, d_attn:d_attn + n_gate]
    w_z = w_qg[:, d_attn + n_gate:]
    gpad = jnp.zeros((d, GATE_ROWS - n_gate), w_qg.dtype)
    wqzt = jnp.concatenate([w_q, w_z, w_g, gpad], axis=1).T.astype(BF16)
    return dict(wrow=wrow, wkvt=w_kv.T.astype(BF16), wqzt=wqzt)


def _proj_prompt(h1, gkv, gq, pw, batch, seq):
    n, d = h1.shape
    t = TOKEN_TILE_PROJ
    assert seq % t == 0 and t == Q_TILE and seq // SEL_BLOCK <= V7X_LANES - HEAD_DIM
    d_attn = N_HEADS * HEAD_DIM
    nt = n // t
    tps = seq // t
    tok = lambda w: pl.BlockSpec((t, w), lambda i: (i, 0))
    kvt = pl.BlockSpec((1, ROW_WIDTH, t), lambda i: (i // tps, 0, i % tps))
    ktile = pl.BlockSpec((1, KV_HEADS, t, V7X_LANES), lambda i: (i, 0, 0, 0))
    vtile = pl.BlockSpec((1, KV_HEADS, V_ROWS, t), lambda i: (i, 0, 0, 0))
    col = lambda r: pl.BlockSpec((r, t), lambda i: (0, i))
    kvt_shape = jax.ShapeDtypeStruct((batch, ROW_WIDTH, seq), F32)
    return pl.pallas_call(
        functools.partial(_proj_prompt_kernel, tiles_per_seq=tps),
        grid=(nt,),
        in_specs=[tok(d), _full(gkv), _full(gq), _full(pw["wrow"]), _full(pw["wkvt"]), _full(pw["wqzt"])],
        out_specs=[kvt, kvt, kvt, pl.BlockSpec((ROW_WIDTH // V7X_LANES, t, V7X_LANES), lambda i: (0, i, 0)),
                   ktile, ktile, vtile, vtile,
                   col(d_attn), col(d_attn), col(GATE_ROWS)],
        out_shape=[
            kvt_shape, kvt_shape, kvt_shape,
            jax.ShapeDtypeStruct((ROW_WIDTH // V7X_LANES, n, V7X_LANES), F32),
            jax.ShapeDtypeStruct((nt, KV_HEADS, t, V7X_LANES), BF16),
            jax.ShapeDtypeStruct((nt, KV_HEADS, t, V7X_LANES), BF16),
            jax.ShapeDtypeStruct((nt, KV_HEADS, V_ROWS, t), BF16),
            jax.ShapeDtypeStruct((nt, KV_HEADS, V_ROWS, t), BF16),
            jax.ShapeDtypeStruct((d_attn, n), BF16),
            jax.ShapeDtypeStruct((d_attn, n), BF16),
            jax.ShapeDtypeStruct((GATE_ROWS, n), F32),
        ],
        compiler_params=_params(("arbitrary",)),
        name="proj_prompt",
    )(h1, gkv, gq, pw["wrow"], pw["wkvt"], pw["wqzt"])


def _proj_step(h1, gkv, gq, pw):
    n, d = h1.shape
    d_attn = N_HEADS * HEAD_DIM
    ins = [h1, gkv, gq, pw["wkvt"], pw["wqzt"]]
    outs = [N_BRANCH * ROW_WIDTH, d_attn, d_attn, GATE_ROWS]
    return pl.pallas_call(
        _proj_step_kernel,
        grid=(1,),
        in_specs=[_full(a) for a in ins],
        out_specs=[pl.BlockSpec((n, w), lambda i: (0, 0)) for w in outs],
        out_shape=[jax.ShapeDtypeStruct((n, w), F32) for w in outs],
        compiler_params=_params(("arbitrary",)),
        name="proj_step",
    )(*ins)


CHUNK_FEATS = CMP_STRIDE * ROW_WIDTH
AB_WIDTH = 2 * 2 * 2 * V7X_LANES


def _compress_cols(get_cols, w_ref):
    outs = []
    for c in range(2):
        for kp in range(KV_HEADS // 2):
            base = c * KV_WIDTH + kp * V7X_LANES
            lhs = jnp.concatenate([get_cols(r, base).astype(BF16) for r in range(CMP_STRIDE)], axis=1)
            outs.append(_dot(lhs, w_ref[c]))
    return jnp.concatenate(outs, axis=1)


def _compress_rows_kernel(x_ref, w_ref, ab_ref):
    m = x_ref.shape[1] // CMP_STRIDE
    ab_ref[0] = _compress_cols(
        lambda r, lo: x_ref[lo // V7X_LANES, pl.ds(r, m, stride=CMP_STRIDE), :], w_ref)


def _compress_paged_kernel(pt_ref, pool_ref, w_ref, ab_ref, buf, rows_s, sem):
    step = pl.program_id(0) * pl.num_programs(1) + pl.program_id(1)
    n_steps = pl.num_programs(0) * pl.num_programs(1)
    pages = buf.shape[1]
    slot = step % 2

    def copies(s, sl):
        return [pltpu.make_async_copy(pool_ref.at[pt_ref[s * pages + p]], buf.at[sl, p], sem.at[sl])
                for p in range(pages)]

    @pl.when(step == 0)
    def _():
        for cp in copies(0, 0):
            cp.start()

    @pl.when(step + 1 < n_steps)
    def _():
        for cp in copies(step + 1, 1 - slot):
            cp.start()

    for cp in copies(step, slot):
        cp.wait()

    def to_rows(p, carry):
        r0 = pl.multiple_of(p * PAGE_SIZE, PAGE_SIZE)
        for c in range(2):
            for kp in range(KV_HEADS // 2):
                tile = buf[slot, p, c, kp * V7X_LANES:(kp + 1) * V7X_LANES, :]
                rows_s[c * (KV_HEADS // 2) + kp, pl.ds(r0, PAGE_SIZE), :] = tile.T
        return carry

    lax.fori_loop(0, pages, to_rows, 0)
    m = pages * (PAGE_SIZE // CMP_STRIDE)
    ab_ref[0] = _compress_cols(
        lambda r, lo: rows_s[lo // V7X_LANES, pl.ds(r, m, stride=CMP_STRIDE), :], w_ref)


def _compress_weights(cmp_pe, cmp_w1, cmp_w2):
    hid = cmp_w1.shape[2]
    w1r = cmp_w1.reshape(2, 2, CMP_STRIDE, HEAD_DIM, hid)
    eye = jnp.eye(2, dtype=cmp_w1.dtype)
    w = jnp.einsum("cardE,kl->crkdalE", w1r, eye)
    w = w.reshape(2, CMP_STRIDE * 2 * HEAD_DIM, 2 * 2 * hid).astype(BF16)
    pe = cmp_pe.reshape(2, 2, CMP_STRIDE, HEAD_DIM).transpose(1, 2, 0, 3)
    pe = jnp.broadcast_to(pe[:, :, :, None, :], (2, CMP_STRIDE, 2, KV_HEADS, HEAD_DIM))
    pe_rows = jnp.zeros((V7X_SUBLANES, CHUNK_FEATS), F32).at[0:2].set(pe.reshape(2, CHUNK_FEATS))
    pe_rows = pe_rows.reshape(V7X_SUBLANES * CMP_STRIDE, ROW_WIDTH // V7X_LANES, V7X_LANES).transpose(1, 0, 2)
    eye4 = jnp.eye(KV_HEADS, dtype=cmp_w2.dtype)
    w2k = jnp.einsum("ed,kl->keld", cmp_w2[0], eye4)
    w2k_pad = jnp.concatenate([w2k, jnp.zeros_like(w2k)], axis=3)
    w2k_pad = w2k_pad.reshape(KV_HEADS * hid, KV_HEADS * V7X_LANES).astype(BF16)
    w2k_bd = w2k.reshape(KV_HEADS * hid, KV_WIDTH).astype(BF16)
    w2v_bd = jnp.einsum("ed,kl->keld", cmp_w2[1], eye4).reshape(KV_HEADS * hid, KV_WIDTH).astype(BF16)
    return dict(w1=w, pe_rows=pe_rows, w2k_pad=w2k_pad, w2k_bd=w2k_bd, w2v_bd=w2v_bd,
                w2vt=w2v_bd.T)


def _compress_rows(x, w1, seq):
    groups, n, lanes = x.shape
    ns = n // seq
    nck = seq // CMP_STRIDE
    return pl.pallas_call(
        _compress_rows_kernel,
        grid=(ns,),
        in_specs=[pl.BlockSpec((groups, seq, lanes), lambda b: (0, b, 0)), _full(w1)],
        out_specs=pl.BlockSpec((1, nck, AB_WIDTH), lambda b: (b, 0, 0)),
        out_shape=jax.ShapeDtypeStruct((ns, nck, AB_WIDTH), F32),
        compiler_params=_params(("arbitrary",)),
        name="compress_rows",
    )(x, w1)


def _compress_paged(page_table, pool, w1):
    ns, n_pages = page_table.shape
    pages = min(CMP_PAGES_PER_STEP, n_pages)
    assert n_pages % pages == 0 and pool.shape[1:] == (2, KV_WIDTH, PAGE_SIZE) and PAGE_SIZE == V7X_LANES
    steps = n_pages // pages
    chunks_per_page = PAGE_SIZE // CMP_STRIDE
    m = pages * chunks_per_page
    return pl.pallas_call(
        _compress_paged_kernel,
        grid_spec=pltpu.PrefetchScalarGridSpec(
            num_scalar_prefetch=1,
            grid=(ns, steps),
            in_specs=[pl.BlockSpec(memory_space=pl.ANY),
                      pl.BlockSpec(w1.shape, lambda b, s, pt: (0, 0, 0))],
            out_specs=pl.BlockSpec((1, m, AB_WIDTH), lambda b, s, pt: (b, s, 0)),
            scratch_shapes=[pltpu.VMEM((2, pages) + pool.shape[1:], F32),
                            pltpu.VMEM((ROW_WIDTH // V7X_LANES, pages * PAGE_SIZE, V7X_LANES), F32),
                            pltpu.SemaphoreType.DMA((2,))],
        ),
        out_shape=jax.ShapeDtypeStruct((ns, n_pages * chunks_per_page, AB_WIDTH), F32),
        compiler_params=_params(("arbitrary", "arbitrary")),
        name="compress_paged",
    )(page_table.reshape(-1), pool, w1)


def _cmp_hidden(ab, bias):
    n = ab.shape[0]
    hid = []
    for grp in range(4):
        base = grp * 2 * V7X_LANES
        a = ab[:, base:base + V7X_LANES] + bias[0:1, base:base + V7X_LANES]
        b = ab[:, base + V7X_LANES:base + 2 * V7X_LANES] + bias[1:2, base + V7X_LANES:base + 2 * V7X_LANES]
        hid.append(_silu(a + pltpu.roll(b, n - 1, 0)))
    return (jnp.concatenate(hid[0:2], axis=1).astype(BF16),
            jnp.concatenate(hid[2:4], axis=1).astype(BF16))


def _cmp_tokens_prompt_kernel(ab_ref, bias_ref, w2k_ref, w2vt_ref, kc_ref, vtc_ref):
    hk, hv = _cmp_hidden(ab_ref[0], bias_ref[0])
    k = _dot(hk, w2k_ref[...]).astype(BF16)
    vt = _dot_nt(w2vt_ref[...], hv).astype(BF16)
    for h in range(KV_HEADS):
        kc_ref[0, h] = k[:, h * V7X_LANES:(h + 1) * V7X_LANES]
        vtc_ref[0, h] = vt[h * HEAD_DIM:(h + 1) * HEAD_DIM]


def _cmp_tokens_prompt(ab, bias, cw):
    ns, nck, _ = ab.shape
    return pl.pallas_call(
        _cmp_tokens_prompt_kernel,
        grid=(ns,),
        in_specs=[pl.BlockSpec((1, nck, AB_WIDTH), lambda b: (b, 0, 0)),
                  _full(bias), _full(cw["w2k_pad"]), _full(cw["w2vt"])],
        out_specs=[pl.BlockSpec((1, KV_HEADS, nck, V7X_LANES), lambda b: (b, 0, 0, 0)),
                   pl.BlockSpec((1, KV_HEADS, HEAD_DIM, nck), lambda b: (b, 0, 0, 0))],
        out_shape=[jax.ShapeDtypeStruct((ns, KV_HEADS, nck, V7X_LANES), BF16),
                   jax.ShapeDtypeStruct((ns, KV_HEADS, HEAD_DIM, nck), BF16)],
        compiler_params=_params(("arbitrary",)),
        name="cmp_tokens_prompt",
    )(ab, bias, cw["w2k_pad"], cw["w2vt"])


def _topk_pick(score, k, axis):
    n = score.shape[axis]
    idx = lax.broadcasted_iota(jnp.int32, score.shape, axis).astype(F32)
    s = score
    for _ in range(k):
        m = jnp.max(s, axis=axis, keepdims=True)
        first = jnp.min(jnp.where(s == m, idx, float(n)), axis=axis, keepdims=True)
        s = jnp.where(idx == first, -jnp.inf, s)
        yield m, first.astype(jnp.int32)


def _block_scores(imp, blk, pos, n_blocks):
    valid = blk * SEL_BLOCK <= pos
    cur = pos // SEL_BLOCK
    forced = (blk == 0) | (blk == cur) | (blk == cur - 1)
    score = jnp.where(valid & forced, FORCE, jnp.where(valid, imp, -FORCE))
    return jnp.where(blk < n_blocks, score, -jnp.inf)


def _block_members(ci, blk, n_cmp, n_blocks):
    ratio = SEL_BLOCK // CMP_STRIDE
    return ((ci >= ratio * blk - (CMP_LEN // CMP_STRIDE - 1)) & (ci <= ratio * blk + ratio - 1)
            & (ci < n_cmp) & (blk < n_blocks))


def _softmax_tile(state, s, vt):
    m, acc = state
    m_new = jnp.maximum(m, jnp.max(s, axis=0, keepdims=True))
    p = jnp.exp2(s - m_new)
    return m_new, jnp.exp2(m - m_new) * acc + _dot(vt, p.astype(BF16))


def _softmax_finish(state):
    _, acc = state
    return acc[0:HEAD_DIM] * (1.0 / jnp.maximum(acc[HEAD_DIM:HEAD_DIM + 1], 1e-30))


def _attn_prompt_kernel(qt_ref, szt_ref, gt_ref, h1_ref, ksel_ref, vts_ref, kwin_ref, vtw_ref,
                        kc_ref, vtc_ref, wo_ref, gf_ref, y_ref, qa_s, ocmp_s, osz_s, *, n_cmp, n_blocks):
    c = pl.program_id(1)
    tq = qt_ref.shape[1]
    ncp = kc_ref.shape[2]
    nbp = V7X_LANES - HEAD_DIM
    pos = c * tq + lax.broadcasted_iota(jnp.int32, (1, tq), 1)
    pos_rows = jnp.concatenate([pos] * GROUP, axis=1)

    kj = lax.broadcasted_iota(jnp.int32, (tq, tq), 0)
    qi = lax.broadcasted_iota(jnp.int32, (tq, tq), 1)
    causal = jnp.where(kj <= qi, 0.0, NEG)
    lower = jnp.where(kj >= qi, jnp.where(c >= 2, 0.0, NEG), NEG)
    has1 = jnp.where(c >= 1, 0.0, NEG)

    tok = lax.broadcasted_iota(jnp.int32, (ncp, 1), 0)
    cmp_mask = (CMP_STRIDE * tok + (CMP_LEN - 1) <= pos_rows) & (tok < n_cmp)
    blk = lax.broadcasted_iota(jnp.int32, (nbp, 1), 0)
    ci = lax.broadcasted_iota(jnp.int32, (1, ncp), 1)
    member_t = _block_members(ci, blk, n_cmp, n_blocks).astype(BF16)
    k_top = min(SEL_TOPK, n_blocks)

    for kvh in range(KV_HEADS):
        q_h = [qt_ref[(kvh * GROUP + g) * HEAD_DIM:(kvh * GROUP + g + 1) * HEAD_DIM, :] for g in range(GROUP)]
        q_t = jnp.concatenate(q_h, axis=1)
        qa0 = jnp.concatenate([q_t, jnp.zeros_like(q_t)], axis=0)
        sc = jnp.where(cmp_mask, _dot(kc_ref[0, kvh], qa0), NEG)
        e = jnp.exp2(sc - jnp.max(sc, axis=0, keepdims=True)) * cmp_mask.astype(F32)
        pc = e * (1.0 / jnp.maximum(jnp.sum(e, axis=0, keepdims=True), 1e-30))
        ocmp_s[kvh] = _dot(vtc_ref[0, kvh], pc.astype(BF16))
        psum = pc[:, 0:tq]
        for g in range(1, GROUP):
            psum = psum + pc[:, g * tq:(g + 1) * tq]
        imp = _dot(member_t, psum.astype(BF16))
        score = _block_scores(imp, blk, pos, n_blocks)
        chosen = jnp.zeros((nbp, tq), jnp.int32)
        for top_s, first in _topk_pick(score, k_top, 0):
            chosen = jnp.where((blk == first) & (top_s > -0.5 * FORCE), 1, chosen)
        mask_t = jnp.where(chosen > 0, 0.0, NEG).astype(BF16)
        for g in range(GROUP):
            qa_s[kvh * GROUP + g] = jnp.concatenate([q_h[g], mask_t], axis=0)

    def tile(k_ref, v_ref, kt, bias, states):
        scores = []
        for kvh in range(KV_HEADS):
            keys = k_ref[kt, kvh]
            for g in range(GROUP):
                s = _dot(keys, qa_s[kvh * GROUP + g])
                scores.append(s if bias is None else s + bias)
        out = []
        for kvh in range(KV_HEADS):
            vt = v_ref[kt, kvh]
            for g in range(GROUP):
                h = kvh * GROUP + g
                out.append(_softmax_tile(states[h], scores[h], vt))
        return tuple(out)

    init = tuple((jnp.full((1, tq), NEG, F32), jnp.zeros((V_ROWS, tq), F32)) for _ in range(N_HEADS))
    st = tile(ksel_ref, vts_ref, c, causal, init)
    st = lax.fori_loop(0, c, lambda kt, s: tile(ksel_ref, vts_ref, kt, None, s), st)
    o_sel = [_softmax_finish(s) for s in st]
    st = tile(kwin_ref, vtw_ref, c, causal, init)
    st = tile(kwin_ref, vtw_ref, jnp.maximum(c - 1, 0), has1, st)
    st = tile(kwin_ref, vtw_ref, jnp.maximum(c - 2, 0), lower, st)
    o_win = [_softmax_finish(s) for s in st]

    for h in range(N_HEADS):
        kvh, g = divmod(h, GROUP)
        gate = lambda br: gt_ref[h * N_BRANCH + br:h * N_BRANCH + br + 1, :]
        o = gate(0) * ocmp_s[kvh, :, g * tq:(g + 1) * tq] + gate(1) * o_sel[h] + gate(2) * o_win[h]
        rows = slice(h * HEAD_DIM, (h + 1) * HEAD_DIM)
        osz_s[rows, :] = (o * szt_ref[rows, :].astype(F32)).astype(BF16)
    y = lax.dot_general(osz_s[...], wo_ref[...], _TN, preferred_element_type=F32)
    y_ref[...] = _rmsnorm(h1_ref[...] + y, gf_ref[...])


def _attn_prompt(qt, szt, gt, h1, ksel, vts, kwin, vtw, kc, vtc, wo, gf, batch, seq):
    n, d = h1.shape
    tq = Q_TILE
    nch = seq // tq
    d_attn = N_HEADS * HEAD_DIM
    n_cmp = seq // CMP_STRIDE - 1
    n_blocks = -(-seq // SEL_BLOCK)
    col = lambda r: pl.BlockSpec((r, tq), lambda b, c: (0, b * nch + c))
    seq_k = pl.BlockSpec((nch, KV_HEADS, tq, V7X_LANES), lambda b, c: (b, 0, 0, 0))
    seq_v = pl.BlockSpec((nch, KV_HEADS, V_ROWS, tq), lambda b, c: (b, 0, 0, 0))
    ncp = kc.shape[2]
    return pl.pallas_call(
        functools.partial(_attn_prompt_kernel, n_cmp=n_cmp, n_blocks=n_blocks),
        grid=(batch, nch),
        in_specs=[col(d_attn), col(d_attn), col(GATE_ROWS),
                  pl.BlockSpec((tq, d), lambda b, c: (b * nch + c, 0)),
                  seq_k, seq_v, seq_k, seq_v,
                  pl.BlockSpec((1, KV_HEADS, ncp, V7X_LANES), lambda b, c: (b, 0, 0, 0)),
                  pl.BlockSpec((1, KV_HEADS, HEAD_DIM, ncp), lambda b, c: (b, 0, 0, 0)),
                  _full(wo), _full(gf)],
        out_specs=pl.BlockSpec((tq, d), lambda b, c: (b * nch + c, 0)),
        out_shape=jax.ShapeDtypeStruct((n, d), F32),
        scratch_shapes=[pltpu.VMEM((N_HEADS, V7X_LANES, tq), BF16),
                        pltpu.VMEM((KV_HEADS, HEAD_DIM, GROUP * tq), F32),
                        pltpu.VMEM((d_attn, tq), BF16)],
        compiler_params=_params(("arbitrary", "arbitrary")),
        name="attn_prompt",
    )(qt, szt, gt, h1, ksel, vts, kwin, vtw, kc, vtc, wo, gf)


HEAD_ROWS = KV_HEADS * V7X_SUBLANES


def _attn_step_cmp_kernel(ab_ref, bias_ref, w2k_ref, w2v_ref, qbd_ref, og_ref, idx_ref,
                          *, n_cmp, n_blocks, pos):
    nck = ab_ref.shape[1]
    nbp = idx_ref.shape[2] * (-(-n_blocks // idx_ref.shape[2]))
    hk, hv = _cmp_hidden(ab_ref[0], bias_ref[0])
    k = _dot(hk, w2k_ref[...]).astype(BF16)
    v = _dot(hv, w2v_ref[...]).astype(BF16)
    tok = lax.broadcasted_iota(jnp.int32, (1, nck), 1)
    mask = (CMP_STRIDE * tok + (CMP_LEN - 1) <= pos) & (tok < n_cmp)
    s = jnp.where(mask, _dot_nt(qbd_ref[0].astype(BF16), k), NEG)
    e = jnp.exp(s - jnp.max(s, axis=1, keepdims=True)) * mask.astype(F32)
    p = e * (1.0 / jnp.maximum(jnp.sum(e, axis=1, keepdims=True), 1e-30))
    r = lax.broadcasted_iota(jnp.int32, (HEAD_ROWS, 1), 0)
    p = jnp.where(r % V7X_SUBLANES < GROUP, p, 0.0)
    o = _dot(p.astype(BF16), v)
    lane = lax.broadcasted_iota(jnp.int32, (V7X_SUBLANES, KV_WIDTH), 1)
    og = jnp.zeros((V7X_SUBLANES, KV_WIDTH), F32)
    for h in range(KV_HEADS):
        og = jnp.where(lane // HEAD_DIM == h, o[h * V7X_SUBLANES:(h + 1) * V7X_SUBLANES], og)
    og_ref[0] = og
    r8 = lax.broadcasted_iota(jnp.int32, (V7X_SUBLANES, 1), 0)
    psum = jnp.zeros((V7X_SUBLANES, nck), F32)
    for h in range(KV_HEADS):
        ph = jnp.sum(p[h * V7X_SUBLANES:(h + 1) * V7X_SUBLANES], axis=0, keepdims=True)
        psum = jnp.where(r8 == h, ph, psum)
    ci = lax.broadcasted_iota(jnp.int32, (nck, 1), 0)
    blk = lax.broadcasted_iota(jnp.int32, (1, nbp), 1)
    member = _block_members(ci, blk, n_cmp, n_blocks).astype(BF16)
    imp = _dot(psum.astype(BF16), member)
    score = _block_scores(imp, blk, pos, n_blocks)
    slot = lax.broadcasted_iota(jnp.int32, idx_ref.shape[1:], 1)
    picked = jnp.full(idx_ref.shape[1:], -1, jnp.int32)
    for j, (top_s, first) in enumerate(_topk_pick(score, min(SEL_TOPK, n_blocks), 1)):
        picked = jnp.where(slot == j, jnp.where(top_s > -0.5 * FORCE, first, -1), picked)
    idx_ref[0] = picked


def _attn_step_cmp(ab, bias, cw, qbd, n_cmp, n_blocks, pos):
    ns, nck, _ = ab.shape
    return pl.pallas_call(
        functools.partial(_attn_step_cmp_kernel, n_cmp=n_cmp, n_blocks=n_blocks, pos=pos),
        grid=(ns,),
        in_specs=[pl.BlockSpec((1, nck, AB_WIDTH), lambda b: (b, 0, 0)),
                  _full(bias), _full(cw["w2k_bd"]), _full(cw["w2v_bd"]),
                  pl.BlockSpec((1,) + qbd.shape[1:], lambda b: (b, 0, 0))],
        out_specs=[pl.BlockSpec((1, V7X_SUBLANES, KV_WIDTH), lambda b: (b, 0, 0)),
                   pl.BlockSpec((1, V7X_SUBLANES, V7X_LANES), lambda b: (b, 0, 0))],
        out_shape=[jax.ShapeDtypeStruct((ns, V7X_SUBLANES, KV_WIDTH), F32),
                   jax.ShapeDtypeStruct((ns, V7X_SUBLANES, V7X_LANES), jnp.int32)],
        compiler_params=_params(("arbitrary",)),
        name="attn_step_cmp",
    )(ab, bias, cw["w2k_bd"], cw["w2v_bd"], qbd)


def _attn_step_kernel(idx_ref, pt_ref, pool_ref, q_ref, new_ref, win_ref, osel_ref, owin_ref,
                      buf, sem, *, n_past_blocks, pos, win_pos0):
    b = pl.program_id(0)
    nb = pl.num_programs(0)
    per_head = buf.shape[4] // PAGE_SIZE
    n_slots = KV_HEADS * per_head
    pages_per_seq = pt_ref.shape[0] // nb
    halves = PAGE_SIZE // SEL_BLOCK
    slot = b % 2

    def block_of(bb, h, j):
        return idx_ref[bb * n_slots + h * per_head + j]

    def copies(bb, sl):
        out = []
        for h in range(KV_HEADS):
            for j in range(per_head):
                blk = jnp.clip(block_of(bb, h, j), 0, n_past_blocks - 1)
                page = pt_ref[bb * pages_per_seq + blk // halves]
                out.append(pltpu.make_async_copy(
                    pool_ref.at[page, :, h], buf.at[sl, h, :, :, pl.ds(j * PAGE_SIZE, PAGE_SIZE)], sem.at[sl]))
        return out

    @pl.when(b == 0)
    def _():
        for cp in copies(0, 0):
            cp.start()

    @pl.when(b + 1 < nb)
    def _():
        for cp in copies(b + 1, 1 - slot):
            cp.start()

    def with_new_token(q, s, vt16, k_new, v_new, new_ok):
        s_new = jnp.where(new_ok, jnp.sum(q * k_new, axis=1, keepdims=True), NEG)
        m = jnp.maximum(jnp.max(s, axis=1, keepdims=True), s_new)
        e = jnp.exp(s - m)
        e_new = jnp.where(new_ok, jnp.exp(s_new - m), 0.0)
        den = jnp.maximum(jnp.sum(e, axis=1, keepdims=True) + e_new, 1e-30)
        return (_dot_nt(e.astype(BF16), vt16) + e_new * v_new) * (1.0 / den)

    def new_row(branch, kv, h):
        r = (branch * 2 + kv) * KV_HEADS + h
        return new_ref[0, r:r + 1, :]

    n_win = win_ref.shape[4]
    wpos = win_pos0 + lax.broadcasted_iota(jnp.int32, (1, n_win), 1)
    wmask = (wpos >= 0) & (pos - wpos >= 0) & (pos - wpos <= WINDOW)
    for h in range(KV_HEADS):
        q = q_ref[0, h]
        s = jnp.where(wmask, _dot(q.astype(BF16), win_ref[0, 0, h].astype(BF16)), NEG)
        owin_ref[0, h] = with_new_token(q, s, win_ref[0, 1, h].astype(BF16),
                                        new_row(2, 0, h), new_row(2, 1, h), True)

    for cp in copies(b, slot):
        cp.wait()

    lane = lax.broadcasted_iota(jnp.int32, (1, per_head * PAGE_SIZE), 1)
    lane_slot = lane // PAGE_SIZE
    lane_half = (lane % PAGE_SIZE) // SEL_BLOCK
    for h in range(KV_HEADS):
        code = jnp.zeros((1, per_head * PAGE_SIZE), jnp.int32)
        has_new = jnp.int32(0)
        for j in range(per_head):
            blk = block_of(b, h, j)
            past = (blk >= 0) & (blk < n_past_blocks)
            code = jnp.where(lane_slot == j, jnp.where(past, 1 + blk % halves, 0), code)
            has_new = has_new | ((blk >= n_past_blocks) & (blk * SEL_BLOCK <= pos)).astype(jnp.int32)
        q = q_ref[0, h]
        s = jnp.where(code == 1 + lane_half, _dot(q.astype(BF16), buf[slot, h, 0].astype(BF16)), NEG)
        osel_ref[0, h] = with_new_token(q, s, buf[slot, h, 1].astype(BF16),
                                        new_row(1, 0, h), new_row(1, 1, h), has_new > 0)


def _attn_step(idx, page_table, pool_t, q8, new_rows, win_t, n_past_blocks, pos, win_pos0):
    ns = q8.shape[0]
    per_head = min(SEL_TOPK, n_past_blocks + 1)
    blk = lambda a: pl.BlockSpec((1,) + a.shape[1:], lambda b, i, p: (b,) + (0,) * (a.ndim - 1))
    out = pl.BlockSpec((1, KV_HEADS, V7X_SUBLANES, HEAD_DIM), lambda b, i, p: (b, 0, 0, 0))
    out_shape = jax.ShapeDtypeStruct((ns, KV_HEADS, V7X_SUBLANES, HEAD_DIM), F32)
    return pl.pallas_call(
        functools.partial(_attn_step_kernel, n_past_blocks=n_past_blocks, pos=pos, win_pos0=win_pos0),
        grid_spec=pltpu.PrefetchScalarGridSpec(
            num_scalar_prefetch=2,
            grid=(ns,),
            in_specs=[pl.BlockSpec(memory_space=pl.ANY), blk(q8), blk(new_rows), blk(win_t)],
            out_specs=[out, out],
            scratch_shapes=[pltpu.VMEM((2, KV_HEADS, 2, HEAD_DIM, per_head * PAGE_SIZE), F32),
                            pltpu.SemaphoreType.DMA((2,))],
        ),
        out_shape=[out_shape, out_shape],
        compiler_params=_params(("arbitrary",)),
        name="attn_step",
    )(idx, page_table.reshape(-1), pool_t, q8, new_rows, win_t)


def _out_step_kernel(oc_ref, os_ref, ow_ref, g0_ref, g1_ref, g2_ref, sz_ref, h1_ref, wo_ref, gf_ref, y_ref):
    o = g0_ref[...] * oc_ref[...] + g1_ref[...] * os_ref[...] + g2_ref[...] * ow_ref[...]
    y = _dot((o * sz_ref[...]).astype(BF16), wo_ref[...])
    y_ref[...] = _rmsnorm(h1_ref[...] + y, gf_ref[...])


def _out_step(*ins):
    h1 = ins[7]
    return pl.pallas_call(
        _out_step_kernel,
        grid=(1,),
        in_specs=[_full(a) for a in ins],
        out_specs=_full(h1),
        out_shape=jax.ShapeDtypeStruct(h1.shape, F32),
        compiler_params=_params(("arbitrary",)),
        name="out_step",
    )(*ins)


def kernel(x_prompt, x_sample, cache_conv, cache_cmp_kv, cache_sel_kv, cache_win_kv, page_table,
           norm_a, conv_w_in, conv_w, conv_w_out, norm_kv, w_kv, cmp_pe, cmp_w1, cmp_w2,
           norm_b, w_qg, w_o, norm_f):
    batch, seq, d = x_prompt.shape
    dec, dec_seq, _ = x_sample.shape
    assert norm_a.shape[0] == 1 and norm_b.shape[0] == 1 and dec_seq == 1
    assert cache_conv.shape[2] == CONV_WIDTH - 1 and cache_cmp_kv.shape[1] == PAGE_SIZE
    n_pool = cache_cmp_kv.shape[0]
    n_pages = page_table.shape[1]
    past_len = n_pages * PAGE_SIZE
    dc = conv_w.shape[2]
    d_attn = N_HEADS * HEAD_DIM

    win_r, wout = _conv_weights(conv_w_in[0], conv_w_out[0])
    pw = _proj_weights(w_kv, w_qg[0])
    cw = _compress_weights(cmp_pe, cmp_w1, cmp_w2)
    g_a, g_kv, g_b, g_f = norm_a[0][None], norm_kv[None], norm_b[0][None], norm_f[None]
    wo16 = w_o[0].astype(BF16)
    bias = _compress_rows(cw["pe_rows"], cw["w1"], cw["pe_rows"].shape[1])
    to_tiles = lambda a: a.transpose(0, 2, 3, 4, 1)
    from_tiles = lambda a: a.reshape(a.shape[0], 2, KV_HEADS, HEAD_DIM, a.shape[2]).transpose(0, 4, 1, 2, 3)

    n = batch * seq
    h1_p, conv_p = _conv_prompt(x_prompt.reshape(n, d), g_a, win_r, conv_w[0], wout, seq)
    conv_p = conv_p.transpose(0, 2, 1, 3).reshape(batch, CONV_WIDTH - 1, dc)
    (kvt_c, kvt_s, kvt_w, kcmp, ksel, kwin, vts, vtw, qt, szt, gt) = _proj_prompt(
        h1_p, g_kv, g_b, pw, batch, seq)
    ab_p = _compress_rows(kcmp, cw["w1"], seq)
    kc, vtc = _cmp_tokens_prompt(ab_p, bias, cw)
    y_p = _attn_prompt(qt, szt, gt, h1_p, ksel, vts, kwin, vtw, kc, vtc, wo16, g_f, batch, seq)

    prev = cache_conv[0]
    h1_s, u_s = _conv_step(x_sample.reshape(dec, d), g_a, win_r, conv_w[0], wout, prev[:, 0], prev[:, 1])
    kv_s, q_s, sz_s, gate_s = _proj_step(h1_s, g_kv, g_b, pw)
    ab_s = _compress_paged(page_table, to_tiles(cache_cmp_kv).reshape(n_pool, 2, KV_WIDTH, PAGE_SIZE),
                           cw["w1"])
    n_cmp_s = (past_len + dec_seq) // CMP_STRIDE - 1
    assert n_cmp_s + 1 == ab_s.shape[1]
    n_blocks_s = -(-(past_len + dec_seq) // SEL_BLOCK)
    q8 = jnp.pad(q_s.reshape(dec, KV_HEADS, GROUP, HEAD_DIM),
                 ((0, 0), (0, 0), (0, V7X_SUBLANES - GROUP), (0, 0)))
    qbd = jnp.einsum("bkgd,kl->bkgld", q8, jnp.eye(KV_HEADS, dtype=F32)).reshape(dec, HEAD_ROWS, KV_WIDTH)
    ogc, idx = _attn_step_cmp(ab_s, bias, cw, qbd, n_cmp_s, n_blocks_s, past_len)
    k_sel = min(SEL_TOPK, n_blocks_s)
    idx = idx[:, :KV_HEADS, :k_sel].reshape(-1)
    n_win = cache_win_kv.shape[1]
    o_sel, o_win = _attn_step(idx, page_table, to_tiles(cache_sel_kv), q8,
                              kv_s.reshape(dec, N_BRANCH * 2 * KV_HEADS, HEAD_DIM), to_tiles(cache_win_kv),
                              past_len // SEL_BLOCK, past_len, past_len - n_win)
    heads = lambda o: o[:, :, :GROUP].reshape(dec, d_attn)
    o_cmp = ogc[:, :GROUP].reshape(dec, GROUP, KV_HEADS, HEAD_DIM).transpose(0, 2, 1, 3).reshape(dec, d_attn)
    gates = gate_s[:, :N_BRANCH * N_HEADS].reshape(dec, N_HEADS, N_BRANCH)
    g_br = [jnp.repeat(gates[:, :, br], HEAD_DIM, axis=1) for br in range(N_BRANCH)]
    y_s = _out_step(o_cmp, heads(o_sel), heads(o_win), g_br[0], g_br[1], g_br[2], sz_s, h1_s, wo16, g_f)

    kv5 = lambda a, lead: a.reshape(lead + (2, KV_HEADS, HEAD_DIM))
    keep_p = min(WINDOW, seq)
    kvs3 = kv_s.reshape(dec, N_BRANCH, ROW_WIDTH)
    win_all = jnp.concatenate([cache_win_kv, kv5(kvs3[:, 2], (dec, 1))], axis=1)
    keep_s = min(WINDOW, win_all.shape[1])
    conv_s = jnp.stack([prev[:, 1], u_s], axis=1)[None]
    return (y_p.reshape(batch, seq, d), y_s.reshape(dec, 1, d), conv_p[None], conv_s,
            from_tiles(kvt_c), kv5(kvs3[:, 0], (dec, 1)),
            from_tiles(kvt_s), kv5(kvs3[:, 1], (dec, 1)),
            from_tiles(kvt_w[:, :, seq - keep_p:]), win_all[:, win_all.shape[1] - keep_s:])
```

```python
import functools
import math

import jax
import jax.numpy as jnp
from jax import lax
from jax.experimental import pallas as pl
from jax.experimental.pallas import tpu as pltpu

F32 = jnp.float32
BF16 = jnp.bfloat16

N_HEADS = 16
HEAD_DIM = 64
KV_HEADS = 4
GROUP = N_HEADS // KV_HEADS
KV_WIDTH = KV_HEADS * HEAD_DIM
ROW_WIDTH = 2 * KV_WIDTH
N_BRANCH = 3
CONV_WIDTH = 3
CMP_STRIDE = 16
CMP_LEN = 2 * CMP_STRIDE
SEL_BLOCK = 64
SEL_TOPK = 16
WINDOW = 512
PAGE_SIZE = 128
EPS = 1e-6
NEG = -1e30
FORCE = 1e9
SCALE = HEAD_DIM ** -0.5
LOG2E = math.log2(math.e)

V7X_LANES = 128
V7X_SUBLANES = 8
V7X_BF16_ROWS = 16
V7X_VMEM_LIMIT_BYTES = 56 * 1024 * 1024

CONV_CHUNK = 256
TOKEN_TILE_CONV = 512
TOKEN_TILE_PROJ = 256
Q_TILE = 256
CMP_PAGES_PER_STEP = 32
REGROUP_UNROLL = 4
GATE_ROWS = 128
V_ROWS = HEAD_DIM + V7X_BF16_ROWS

_NT = (((1,), (1,)), ((), ()))
_TN = (((0,), (0,)), ((), ()))


def _params(sem):
    return pltpu.CompilerParams(dimension_semantics=sem, vmem_limit_bytes=V7X_VMEM_LIMIT_BYTES)


def _rmsnorm(x, g):
    return x * lax.rsqrt(jnp.mean(x * x, axis=-1, keepdims=True) + EPS) * g


def _sigmoid(x):
    return 1.0 / (1.0 + jnp.exp(-x))


def _silu(x):
    return x * _sigmoid(x)


def _dot(a, b):
    return jnp.dot(a, b, preferred_element_type=F32)


def _dot_nt(a, b):
    return lax.dot_general(a, b, _NT, preferred_element_type=F32)


def _full(a):
    return pl.BlockSpec(a.shape, lambda *_: (0,) * a.ndim)


def _conv_gate(xn_s, win_ref, cw):
    proj = _dot(xn_s[...], win_ref[0])
    bg, cg, hh, z = (proj[:, k * cw:(k + 1) * cw] for k in range(4))
    return bg, cg * hh, z


def _conv_prompt_kernel(x_ref, g_ref, win_ref, cw_ref, wout_ref, h1_ref, st_ref, ubuf_s, carry_s,
                        *, tiles_per_seq):
    i = pl.program_id(0)
    t = x_ref.shape[0]
    nc = win_ref.shape[0]
    cw = CONV_CHUNK
    lo = V7X_SUBLANES
    x = x_ref[...]
    xn = _rmsnorm(x, g_ref[...]).astype(BF16)

    @pl.when(i % tiles_per_seq == 0)
    def _():
        carry_s[...] = jnp.zeros_like(carry_s)

    def gate(j, proj):
        bg, cg, hh, z = (proj[:, k * cw:(k + 1) * cw] for k in range(4))
        u = cg * hh
        ubuf_s[j, 0:lo, :] = carry_s[j]
        ubuf_s[j, lo:lo + t, :] = u
        w = cw_ref[:, j * cw:(j + 1) * cw]
        v = w[0:1] * ubuf_s[j, lo - 2:lo - 2 + t, :] + w[1:2] * ubuf_s[j, lo - 1:lo - 1 + t, :] + w[2:3] * u
        carry_s[j] = u[t - lo:t]
        st_ref[0, j] = ubuf_s[j, lo + t - 2:lo + t, :]
        return (_silu(z) * bg * v).astype(BF16)

    proj = _dot(xn, win_ref[0])
    y = None
    for j in range(nc):
        nxt = _dot(xn, win_ref[j + 1]) if j + 1 < nc else None
        d = _dot(gate(j, proj), wout_ref[j * cw:(j + 1) * cw, :])
        y = d if y is None else y + d
        proj = nxt
    h1_ref[...] = x + y


def _conv_step_kernel(x_ref, g_ref, win_ref, cw_ref, wout_ref, p0_ref, p1_ref, h1_ref, u_ref,
                      xn_s, acc_s):
    j = pl.program_id(0)
    cw = cw_ref.shape[1]

    @pl.when(j == 0)
    def _():
        xn_s[...] = _rmsnorm(x_ref[...], g_ref[...]).astype(BF16)
        acc_s[...] = jnp.zeros_like(acc_s)

    bg, u, z = _conv_gate(xn_s, win_ref, cw)
    w = cw_ref[...]
    v = w[0:1] * p0_ref[...] + w[1:2] * p1_ref[...] + w[2:3] * u
    u_ref[...] = u
    gated = _silu(z) * bg * v
    acc_s[...] += _dot(gated.astype(BF16), wout_ref[...])

    @pl.when(j == pl.num_programs(0) - 1)
    def _():
        h1_ref[...] = x_ref[...] + acc_s[...]


def _conv_weights(w_in, w_out):
    d, four_dc = w_in.shape
    dc = four_dc // 4
    nc = dc // CONV_CHUNK
    w = w_in.reshape(d, 4, nc, CONV_CHUNK).transpose(2, 0, 1, 3).reshape(nc, d, 4 * CONV_CHUNK)
    return w.astype(BF16), w_out.astype(BF16)


def _conv_prompt(x, g, win_r, conv_w, wout, seq):
    n, d = x.shape
    nc = win_r.shape[0]
    t = min(TOKEN_TILE_CONV, seq)
    assert seq % t == 0 and t % V7X_SUBLANES == 0
    tiles_per_seq = seq // t
    resident = lambda a: pl.BlockSpec(a.shape, lambda i: (0,) * a.ndim, pipeline_mode=pl.Buffered(1))
    return pl.pallas_call(
        functools.partial(_conv_prompt_kernel, tiles_per_seq=tiles_per_seq),
        grid=(n // t,),
        in_specs=[
            pl.BlockSpec((t, d), lambda i: (i, 0)),
            _full(g), resident(win_r), _full(conv_w), resident(wout),
        ],
        out_specs=[
            pl.BlockSpec((t, d), lambda i: (i, 0)),
            pl.BlockSpec((1, nc, CONV_WIDTH - 1, CONV_CHUNK), lambda i: (i // tiles_per_seq, 0, 0, 0)),
        ],
        out_shape=[
            jax.ShapeDtypeStruct((n, d), F32),
            jax.ShapeDtypeStruct((n // seq, nc, CONV_WIDTH - 1, CONV_CHUNK), F32),
        ],
        scratch_shapes=[
            pltpu.VMEM((nc, t + V7X_SUBLANES, CONV_CHUNK), F32),
            pltpu.VMEM((nc, V7X_SUBLANES, CONV_CHUNK), F32),
        ],
        compiler_params=_params(("arbitrary",)),
        name="conv_prompt",
    )(x, g, win_r, conv_w, wout)


def _conv_step(x, g, win_r, conv_w, wout, prev0, prev1):
    n, d = x.shape
    nc = win_r.shape[0]
    dc = nc * CONV_CHUNK
    return pl.pallas_call(
        _conv_step_kernel,
        grid=(nc,),
        in_specs=[
            pl.BlockSpec((n, d), lambda j: (0, 0)),
            pl.BlockSpec((1, d), lambda j: (0, 0)),
            pl.BlockSpec((1, d, 4 * CONV_CHUNK), lambda j: (j, 0, 0)),
            pl.BlockSpec((CONV_WIDTH, CONV_CHUNK), lambda j: (0, j)),
            pl.BlockSpec((CONV_CHUNK, d), lambda j: (j, 0)),
            pl.BlockSpec((n, CONV_CHUNK), lambda j: (0, j)),
            pl.BlockSpec((n, CONV_CHUNK), lambda j: (0, j)),
        ],
        out_specs=[
            pl.BlockSpec((n, d), lambda j: (0, 0)),
            pl.BlockSpec((n, CONV_CHUNK), lambda j: (0, j)),
        ],
        out_shape=[jax.ShapeDtypeStruct((n, d), F32), jax.ShapeDtypeStruct((n, dc), F32)],
        scratch_shapes=[pltpu.VMEM((n, d), BF16), pltpu.VMEM((n, d), F32)],
        compiler_params=_params(("arbitrary",)),
        name="conv_step",
    )(x, g, win_r, conv_w, wout, prev0, prev1)


def _proj_prompt_kernel(h_ref, gkv_ref, gq_ref, wrow_ref, wkvt_ref, wqzt_ref,
                        kvtc_ref, kvts_ref, kvtw_ref, kcmp_ref, ksel_ref, kwin_ref, vts_ref, vtw_ref,
                        qt_ref, szt_ref, gt_ref, *, tiles_per_seq):
    i = pl.program_id(0)
    t = h_ref.shape[0]
    d_attn = N_HEADS * HEAD_DIM
    h = h_ref[...]
    kvn = _rmsnorm(h, gkv_ref[...]).astype(BF16)
    kvt = _dot_nt(wkvt_ref[...], kvn)
    kvtc_ref[0] = kvt[0:ROW_WIDTH]
    kvts_ref[0] = kvt[ROW_WIDTH:2 * ROW_WIDTH]
    kvtw_ref[0] = kvt[2 * ROW_WIDTH:3 * ROW_WIDTH]
    ones = jnp.ones((V_ROWS - HEAD_DIM, t), BF16)
    for k in range(KV_HEADS):
        lo = ROW_WIDTH + KV_WIDTH + k * HEAD_DIM
        vts_ref[0, k, 0:HEAD_DIM] = kvt[lo:lo + HEAD_DIM].astype(BF16)
        vts_ref[0, k, HEAD_DIM:V_ROWS] = ones
        lo = 2 * ROW_WIDTH + KV_WIDTH + k * HEAD_DIM
        vtw_ref[0, k, 0:HEAD_DIM] = kvt[lo:lo + HEAD_DIM].astype(BF16)
        vtw_ref[0, k, HEAD_DIM:V_ROWS] = ones
    row = _dot(kvn, wrow_ref[...])
    for grp in range(ROW_WIDTH // V7X_LANES):
        kcmp_ref[grp] = row[:, grp * V7X_LANES:(grp + 1) * V7X_LANES]
    pos = (i % tiles_per_seq) * t + lax.broadcasted_iota(jnp.int32, (t, V7X_LANES), 0)
    lane = lax.broadcasted_iota(jnp.int32, (t, V7X_LANES), 1)
    onehot = (lane - HEAD_DIM == pos // SEL_BLOCK).astype(F32)
    for k in range(KV_HEADS):
        lo = ROW_WIDTH + k * V7X_LANES
        ksel_ref[0, k] = (row[:, lo:lo + V7X_LANES] + onehot).astype(BF16)
        lo = ROW_WIDTH + (KV_HEADS + k) * V7X_LANES
        kwin_ref[0, k] = row[:, lo:lo + V7X_LANES].astype(BF16)
    qn = _rmsnorm(h, gq_ref[...]).astype(BF16)
    qz = _dot_nt(wqzt_ref[...], qn)
    qt_ref[...] = (qz[0:d_attn] * (SCALE * LOG2E)).astype(BF16)
    szt_ref[...] = _silu(qz[d_attn:2 * d_attn]).astype(BF16)
    gt_ref[...] = _sigmoid(qz[2 * d_attn:2 * d_attn + GATE_ROWS])


def _proj_step_kernel(h_ref, gkv_ref, gq_ref, wkvt_ref, wqzt_ref, kv_ref, q_ref, sz_ref, g_ref):
    d_attn = N_HEADS * HEAD_DIM
    h = h_ref[...]
    kvn = _rmsnorm(h, gkv_ref[...]).astype(BF16)
    kv_ref[...] = _dot_nt(kvn, wkvt_ref[...])
    qn = _rmsnorm(h, gq_ref[...]).astype(BF16)
    qz = _dot_nt(qn, wqzt_ref[...])
    q_ref[...] = qz[:, 0:d_attn] * SCALE
    sz_ref[...] = _silu(qz[:, d_attn:2 * d_attn])
    g_ref[...] = _sigmoid(qz[:, 2 * d_attn:2 * d_attn + GATE_ROWS])


def _proj_weights(w_kv, w_qg):
    d = w_kv.shape[0]
    d_attn = N_HEADS * HEAD_DIM
    n_gate = N_BRANCH * N_HEADS
    wkv4 = w_kv.reshape(d, N_BRANCH, 2, KV_HEADS, HEAD_DIM)
    pad = jnp.zeros((d, KV_HEADS, V7X_LANES - HEAD_DIM), w_kv.dtype)
    ksel = jnp.concatenate([wkv4[:, 1, 0], pad], axis=2).reshape(d, KV_HEADS * V7X_LANES)
    kwin = jnp.concatenate([wkv4[:, 2, 0], pad], axis=2).reshape(d, KV_HEADS * V7X_LANES)
    wrow = jnp.concatenate([w_kv[:, 0:ROW_WIDTH], ksel, kwin], axis=1).astype(BF16)
    w_q = w_qg[:, :d_attn]
    w_g = w_qg[:, d_attn:d_attn + n_gate]
    w_z = w_qg[:, d_attn + n_gate:]
    gpad = jnp.zeros((d, GATE_ROWS - n_gate), w_qg.dtype)
    wqzt = jnp.concatenate([w_q, w_z, w_g, gpad], axis=1).T.astype(BF16)
    return dict(wrow=wrow, wkvt=w_kv.T.astype(BF16), wqzt=wqzt)


def _proj_prompt(h1, gkv, gq, pw, batch, seq):
    n, d = h1.shape
    t = TOKEN_TILE_PROJ
    assert seq % t == 0 and t == Q_TILE and seq // SEL_BLOCK <= V7X_LANES - HEAD_DIM
    d_attn = N_HEADS * HEAD_DIM
    nt = n // t
    tps = seq // t
    tok = lambda w: pl.BlockSpec((t, w), lambda i: (i, 0))
    kvt = pl.BlockSpec((1, ROW_WIDTH, t), lambda i: (i // tps, 0, i % tps))
    ktile = pl.BlockSpec((1, KV_HEADS, t, V7X_LANES), lambda i: (i, 0, 0, 0))
    vtile = pl.BlockSpec((1, KV_HEADS, V_ROWS, t), lambda i: (i, 0, 0, 0))
    col = lambda r: pl.BlockSpec((r, t), lambda i: (0, i))
    kvt_shape = jax.ShapeDtypeStruct((batch, ROW_WIDTH, seq), F32)
    return pl.pallas_call(
        functools.partial(_proj_prompt_kernel, tiles_per_seq=tps),
        grid=(nt,),
        in_specs=[tok(d), _full(gkv), _full(gq), _full(pw["wrow"]), _full(pw["wkvt"]), _full(pw["wqzt"])],
        out_specs=[kvt, kvt, kvt, pl.BlockSpec((ROW_WIDTH // V7X_LANES, t, V7X_LANES), lambda i: (0, i, 0)),
                   ktile, ktile, vtile, vtile,
                   col(d_attn), col(d_attn), col(GATE_ROWS)],
        out_shape=[
            kvt_shape, kvt_shape, kvt_shape,
            jax.ShapeDtypeStruct((ROW_WIDTH // V7X_LANES, n, V7X_LANES), F32),
            jax.ShapeDtypeStruct((nt, KV_HEADS, t, V7X_LANES), BF16),
            jax.ShapeDtypeStruct((nt, KV_HEADS, t, V7X_LANES), BF16),
            jax.ShapeDtypeStruct((nt, KV_HEADS, V_ROWS, t), BF16),
            jax.ShapeDtypeStruct((nt, KV_HEADS, V_ROWS, t), BF16),
            jax.ShapeDtypeStruct((d_attn, n), BF16),
            jax.ShapeDtypeStruct((d_attn, n), BF16),
            jax.ShapeDtypeStruct((GATE_ROWS, n), F32),
        ],
        compiler_params=_params(("arbitrary",)),
        name="proj_prompt",
    )(h1, gkv, gq, pw["wrow"], pw["wkvt"], pw["wqzt"])


def _proj_step(h1, gkv, gq, pw):
    n, d = h1.shape
    d_attn = N_HEADS * HEAD_DIM
    ins = [h1, gkv, gq, pw["wkvt"], pw["wqzt"]]
    outs = [N_BRANCH * ROW_WIDTH, d_attn, d_attn, GATE_ROWS]
    return pl.pallas_call(
        _proj_step_kernel,
        grid=(1,),
        in_specs=[_full(a) for a in ins],
        out_specs=[pl.BlockSpec((n, w), lambda i: (0, 0)) for w in outs],
        out_shape=[jax.ShapeDtypeStruct((n, w), F32) for w in outs],
        compiler_params=_params(("arbitrary",)),
        name="proj_step",
    )(*ins)


CHUNK_FEATS = CMP_STRIDE * ROW_WIDTH
AB_WIDTH = 2 * 2 * 2 * V7X_LANES


def _compress_cols(get_cols, w_ref):
    outs = []
    for c in range(2):
        for kp in range(KV_HEADS // 2):
            base = c * KV_WIDTH + kp * V7X_LANES
            lhs = jnp.concatenate([get_cols(r, base).astype(BF16) for r in range(CMP_STRIDE)], axis=1)
            outs.append(_dot(lhs, w_ref[c]))
    return jnp.concatenate(outs, axis=1)


def _compress_rows_kernel(x_ref, w_ref, ab_ref):
    m = x_ref.shape[1] // CMP_STRIDE
    ab_ref[0] = _compress_cols(
        lambda r, lo: x_ref[lo // V7X_LANES, pl.ds(r, m, stride=CMP_STRIDE), :], w_ref)


def _compress_paged_kernel(pt_ref, pool_ref, perm_ref, w_ref, ab_ref, buf, lhs_s, sem):
    step = pl.program_id(0) * pl.num_programs(1) + pl.program_id(1)
    n_steps = pl.num_programs(0) * pl.num_programs(1)
    pages = buf.shape[1]
    slot = step % 2

    def copies(s, sl):
        return [pltpu.make_async_copy(pool_ref.at[pt_ref[s * pages + p]], buf.at[sl, p], sem.at[sl])
                for p in range(pages)]

    @pl.when(step == 0)
    def _():
        for cp in copies(0, 0):
            cp.start()

    @pl.when(step + 1 < n_steps)
    def _():
        for cp in copies(step + 1, 1 - slot):
            cp.start()

    for cp in copies(step, slot):
        cp.wait()

    pair_chunks = 2 * (PAGE_SIZE // CMP_STRIDE)
    unroll = math.gcd(pages // 2, REGROUP_UNROLL)

    def regroup(it, carry):
        tiles = []
        for u in range(unroll):
            pp = it * unroll + u
            for c in range(2):
                xt = jnp.concatenate([buf[slot, 2 * pp, c], buf[slot, 2 * pp + 1, c]], axis=1).astype(BF16)
                tiles.append((pp, c, _dot_nt(perm_ref[...], xt)))
        for pp, c, t in tiles:
            t = t.astype(BF16)
            m0 = pl.multiple_of(pp * pair_chunks, pair_chunks)
            for r in range(CMP_STRIDE):
                for kp in range(KV_HEADS // 2):
                    lhs_s[c * (KV_HEADS // 2) + kp, pl.ds(m0, pair_chunks), r * V7X_LANES:(r + 1) * V7X_LANES] = (
                        t[r * pair_chunks:(r + 1) * pair_chunks, kp * V7X_LANES:(kp + 1) * V7X_LANES])
        return carry

    lax.fori_loop(0, pages // 2 // unroll, regroup, 0)
    ab_ref[0] = jnp.concatenate(
        [_dot(lhs_s[g], w_ref[g // (KV_HEADS // 2)]) for g in range(lhs_s.shape[0])], axis=1)


def _compress_weights(cmp_pe, cmp_w1, cmp_w2):
    hid = cmp_w1.shape[2]
    w1r = cmp_w1.reshape(2, 2, CMP_STRIDE, HEAD_DIM, hid)
    eye = jnp.eye(2, dtype=cmp_w1.dtype)
    w = jnp.einsum("cardE,kl->crkdalE", w1r, eye)
    w = w.reshape(2, CMP_STRIDE * 2 * HEAD_DIM, 2 * 2 * hid).astype(BF16)
    pe = cmp_pe.reshape(2, 2, CMP_STRIDE, HEAD_DIM).transpose(1, 2, 0, 3)
    pe = jnp.broadcast_to(pe[:, :, :, None, :], (2, CMP_STRIDE, 2, KV_HEADS, HEAD_DIM))
    pe_rows = jnp.zeros((V7X_SUBLANES, CHUNK_FEATS), F32).at[0:2].set(pe.reshape(2, CHUNK_FEATS))
    pe_rows = pe_rows.reshape(V7X_SUBLANES * CMP_STRIDE, ROW_WIDTH // V7X_LANES, V7X_LANES).transpose(1, 0, 2)
    eye4 = jnp.eye(KV_HEADS, dtype=cmp_w2.dtype)
    w2k = jnp.einsum("ed,kl->keld", cmp_w2[0], eye4)
    w2k_pad = jnp.concatenate([w2k, jnp.zeros_like(w2k)], axis=3)
    w2k_pad = w2k_pad.reshape(KV_HEADS * hid, KV_HEADS * V7X_LANES).astype(BF16)
    w2k_bd = w2k.reshape(KV_HEADS * hid, KV_WIDTH).astype(BF16)
    w2v_bd = jnp.einsum("ed,kl->keld", cmp_w2[1], eye4).reshape(KV_HEADS * hid, KV_WIDTH).astype(BF16)
    return dict(w1=w, pe_rows=pe_rows, w2k_pad=w2k_pad, w2k_bd=w2k_bd, w2v_bd=w2v_bd,
                w2vt=w2v_bd.T)


def _compress_rows(x, w1, seq):
    groups, n, lanes = x.shape
    ns = n // seq
    nck = seq // CMP_STRIDE
    return pl.pallas_call(
        _compress_rows_kernel,
        grid=(ns,),
        in_specs=[pl.BlockSpec((groups, seq, lanes), lambda b: (0, b, 0)), _full(w1)],
        out_specs=pl.BlockSpec((1, nck, AB_WIDTH), lambda b: (b, 0, 0)),
        out_shape=jax.ShapeDtypeStruct((ns, nck, AB_WIDTH), F32),
        compiler_params=_params(("arbitrary",)),
        name="compress_rows",
    )(x, w1)


def _compress_paged(page_table, pool, w1):
    ns, n_pages = page_table.shape
    pages = min(CMP_PAGES_PER_STEP, n_pages)
    assert n_pages % pages == 0 and pages % 2 == 0
    assert pool.shape[1:] == (2, KV_WIDTH, PAGE_SIZE) and PAGE_SIZE == V7X_LANES
    steps = n_pages // pages
    chunks_per_page = PAGE_SIZE // CMP_STRIDE
    assert 2 * chunks_per_page == V7X_BF16_ROWS
    m = pages * chunks_per_page
    t = jnp.arange(2 * PAGE_SIZE)
    r, ch = t // (2 * chunks_per_page), t % (2 * chunks_per_page)
    src = (ch // chunks_per_page) * PAGE_SIZE + (ch % chunks_per_page) * CMP_STRIDE + r
    perm = (t[None, :] == src[:, None]).astype(BF16)
    return pl.pallas_call(
        _compress_paged_kernel,
        grid_spec=pltpu.PrefetchScalarGridSpec(
            num_scalar_prefetch=1,
            grid=(ns, steps),
            in_specs=[pl.BlockSpec(memory_space=pl.ANY),
                      pl.BlockSpec(perm.shape, lambda b, s, pt: (0, 0)),
                      pl.BlockSpec(w1.shape, lambda b, s, pt: (0, 0, 0))],
            out_specs=pl.BlockSpec((1, m, AB_WIDTH), lambda b, s, pt: (b, s, 0)),
            scratch_shapes=[pltpu.VMEM((2, pages) + pool.shape[1:], F32),
                            pltpu.VMEM((ROW_WIDTH // V7X_LANES, m, CMP_STRIDE * V7X_LANES), BF16),
                            pltpu.SemaphoreType.DMA((2,))],
        ),
        out_shape=jax.ShapeDtypeStruct((ns, n_pages * chunks_per_page, AB_WIDTH), F32),
        compiler_params=_params(("arbitrary", "arbitrary")),
        name="compress_paged",
    )(page_table.reshape(-1), pool, perm, w1)


def _cmp_hidden(ab, bias):
    n = ab.shape[0]
    hid = []
    for grp in range(4):
        base = grp * 2 * V7X_LANES
        a = ab[:, base:base + V7X_LANES] + bias[0:1, base:base + V7X_LANES]
        b = ab[:, base + V7X_LANES:base + 2 * V7X_LANES] + bias[1:2, base + V7X_LANES:base + 2 * V7X_LANES]
        hid.append(_silu(a + pltpu.roll(b, n - 1, 0)))
    return (jnp.concatenate(hid[0:2], axis=1).astype(BF16),
            jnp.concatenate(hid[2:4], axis=1).astype(BF16))


def _cmp_tokens_prompt_kernel(ab_ref, bias_ref, w2k_ref, w2vt_ref, kc_ref, vtc_ref):
    hk, hv = _cmp_hidden(ab_ref[0], bias_ref[0])
    k = _dot(hk, w2k_ref[...]).astype(BF16)
    vt = _dot_nt(w2vt_ref[...], hv).astype(BF16)
    for h in range(KV_HEADS):
        kc_ref[0, h] = k[:, h * V7X_LANES:(h + 1) * V7X_LANES]
        vtc_ref[0, h] = vt[h * HEAD_DIM:(h + 1) * HEAD_DIM]


def _cmp_tokens_prompt(ab, bias, cw):
    ns, nck, _ = ab.shape
    return pl.pallas_call(
        _cmp_tokens_prompt_kernel,
        grid=(ns,),
        in_specs=[pl.BlockSpec((1, nck, AB_WIDTH), lambda b: (b, 0, 0)),
                  _full(bias), _full(cw["w2k_pad"]), _full(cw["w2vt"])],
        out_specs=[pl.BlockSpec((1, KV_HEADS, nck, V7X_LANES), lambda b: (b, 0, 0, 0)),
                   pl.BlockSpec((1, KV_HEADS, HEAD_DIM, nck), lambda b: (b, 0, 0, 0))],
        out_shape=[jax.ShapeDtypeStruct((ns, KV_HEADS, nck, V7X_LANES), BF16),
                   jax.ShapeDtypeStruct((ns, KV_HEADS, HEAD_DIM, nck), BF16)],
        compiler_params=_params(("arbitrary",)),
        name="cmp_tokens_prompt",
    )(ab, bias, cw["w2k_pad"], cw["w2vt"])


def _topk_pick(score, k, axis):
    n = score.shape[axis]
    idx = lax.broadcasted_iota(jnp.int32, score.shape, axis).astype(F32)
    s = score
    for _ in range(k):
        m = jnp.max(s, axis=axis, keepdims=True)
        first = jnp.min(jnp.where(s == m, idx, float(n)), axis=axis, keepdims=True)
        s = jnp.where(idx == first, -jnp.inf, s)
        yield m, first.astype(jnp.int32)


def _block_scores(imp, blk, pos, n_blocks):
    valid = blk * SEL_BLOCK <= pos
    cur = pos // SEL_BLOCK
    forced = (blk == 0) | (blk == cur) | (blk == cur - 1)
    score = jnp.where(valid & forced, FORCE, jnp.where(valid, imp, -FORCE))
    return jnp.where(blk < n_blocks, score, -jnp.inf)


def _block_members(ci, blk, n_cmp, n_blocks):
    ratio = SEL_BLOCK // CMP_STRIDE
    return ((ci >= ratio * blk - (CMP_LEN // CMP_STRIDE - 1)) & (ci <= ratio * blk + ratio - 1)
            & (ci < n_cmp) & (blk < n_blocks))


def _softmax_tiles(state, scores, vts):
    m, acc = state
    m_new = m
    for s in scores:
        m_new = jnp.maximum(m_new, jnp.max(s, axis=0, keepdims=True))
    acc = jnp.exp2(m - m_new) * acc
    for s, vt in zip(scores, vts):
        acc = acc + _dot(vt, jnp.exp2(s - m_new).astype(BF16))
    return m_new, acc


def _softmax_finish(state):
    _, acc = state
    return acc[0:HEAD_DIM] * (1.0 / jnp.maximum(acc[HEAD_DIM:HEAD_DIM + 1], 1e-30))


def _attn_prompt_kernel(qt_ref, szt_ref, gt_ref, h1_ref, ksel_ref, vts_ref, kwin_ref, vtw_ref,
                        kc_ref, vtc_ref, wo_ref, gf_ref, y_ref, qa_s, ocmp_s, osz_s, *, n_cmp, n_blocks):
    c = pl.program_id(1)
    tq = qt_ref.shape[1]
    ncp = kc_ref.shape[2]
    nbp = V7X_LANES - HEAD_DIM
    pos = c * tq + lax.broadcasted_iota(jnp.int32, (1, tq), 1)
    pos_rows = jnp.concatenate([pos] * GROUP, axis=1)

    kj = lax.broadcasted_iota(jnp.int32, (tq, tq), 0)
    qi = lax.broadcasted_iota(jnp.int32, (tq, tq), 1)
    causal = jnp.where(kj <= qi, 0.0, NEG)
    lower = jnp.where(kj >= qi, jnp.where(c >= 2, 0.0, NEG), NEG)
    has1 = jnp.where(c >= 1, 0.0, NEG)

    tok = lax.broadcasted_iota(jnp.int32, (ncp, 1), 0)
    cmp_mask = (CMP_STRIDE * tok + (CMP_LEN - 1) <= pos_rows) & (tok < n_cmp)
    blk = lax.broadcasted_iota(jnp.int32, (nbp, 1), 0)
    ci = lax.broadcasted_iota(jnp.int32, (1, ncp), 1)
    member_t = _block_members(ci, blk, n_cmp, n_blocks).astype(BF16)
    k_top = min(SEL_TOPK, n_blocks)

    for kvh in range(KV_HEADS):
        q_h = [qt_ref[(kvh * GROUP + g) * HEAD_DIM:(kvh * GROUP + g + 1) * HEAD_DIM, :] for g in range(GROUP)]
        q_t = jnp.concatenate(q_h, axis=1)
        qa0 = jnp.concatenate([q_t, jnp.zeros_like(q_t)], axis=0)
        sc = jnp.where(cmp_mask, _dot(kc_ref[0, kvh], qa0), NEG)
        e = jnp.exp2(sc - jnp.max(sc, axis=0, keepdims=True)) * cmp_mask.astype(F32)
        pc = e * (1.0 / jnp.maximum(jnp.sum(e, axis=0, keepdims=True), 1e-30))
        ocmp_s[kvh] = _dot(vtc_ref[0, kvh], pc.astype(BF16))
        psum = pc[:, 0:tq]
        for g in range(1, GROUP):
            psum = psum + pc[:, g * tq:(g + 1) * tq]
        imp = _dot(member_t, psum.astype(BF16))
        score = _block_scores(imp, blk, pos, n_blocks)
        chosen = jnp.zeros((nbp, tq), jnp.int32)
        for top_s, first in _topk_pick(score, k_top, 0):
            chosen = jnp.where((blk == first) & (top_s > -0.5 * FORCE), 1, chosen)
        mask_t = jnp.where(chosen > 0, 0.0, NEG).astype(BF16)
        for g in range(GROUP):
            qa_s[kvh * GROUP + g] = jnp.concatenate([q_h[g], mask_t], axis=0)

    def tiles(k_ref, v_ref, kts, biases, states):
        scores = []
        for kvh in range(KV_HEADS):
            keys = [k_ref[kt, kvh] for kt in kts]
            for g in range(GROUP):
                qa = qa_s[kvh * GROUP + g]
                scores.append([_dot(k, qa) if b is None else _dot(k, qa) + b for k, b in zip(keys, biases)])
        out = []
        for kvh in range(KV_HEADS):
            vts = [v_ref[kt, kvh] for kt in kts]
            for g in range(GROUP):
                h = kvh * GROUP + g
                out.append(_softmax_tiles(states[h], scores[h], vts))
        return tuple(out)

    init = tuple((jnp.full((1, tq), NEG, F32), jnp.zeros((V_ROWS, tq), F32)) for _ in range(N_HEADS))
    st = lax.fori_loop(
        0, c // 2, lambda j, s: tiles(ksel_ref, vts_ref, (2 * j, 2 * j + 1), (None, None), s), init)
    odd = jnp.where(c % 2 == 1, 0.0, NEG)
    st = tiles(ksel_ref, vts_ref, (jnp.maximum(c - 1, 0), c), (odd, causal), st)
    o_sel = [_softmax_finish(s) for s in st]
    st = tiles(kwin_ref, vtw_ref, (jnp.maximum(c - 2, 0), jnp.maximum(c - 1, 0), c), (lower, has1, causal), init)
    o_win = [_softmax_finish(s) for s in st]

    for h in range(N_HEADS):
        kvh, g = divmod(h, GROUP)
        gate = lambda br: gt_ref[h * N_BRANCH + br:h * N_BRANCH + br + 1, :]
        o = gate(0) * ocmp_s[kvh, :, g * tq:(g + 1) * tq] + gate(1) * o_sel[h] + gate(2) * o_win[h]
        rows = slice(h * HEAD_DIM, (h + 1) * HEAD_DIM)
        osz_s[rows, :] = (o * szt_ref[rows, :].astype(F32)).astype(BF16)
    y = lax.dot_general(osz_s[...], wo_ref[...], _TN, preferred_element_type=F32)
    y_ref[...] = _rmsnorm(h1_ref[...] + y, gf_ref[...])


def _attn_prompt(qt, szt, gt, h1, ksel, vts, kwin, vtw, kc, vtc, wo, gf, batch, seq):
    n, d = h1.shape
    tq = Q_TILE
    nch = seq // tq
    d_attn = N_HEADS * HEAD_DIM
    n_cmp = seq // CMP_STRIDE - 1
    n_blocks = -(-seq // SEL_BLOCK)
    col = lambda r: pl.BlockSpec((r, tq), lambda b, c: (0, b * nch + c))
    seq_k = pl.BlockSpec((nch, KV_HEADS, tq, V7X_LANES), lambda b, c: (b, 0, 0, 0),
                         pipeline_mode=pl.Buffered(1))
    seq_v = pl.BlockSpec((nch, KV_HEADS, V_ROWS, tq), lambda b, c: (b, 0, 0, 0),
                         pipeline_mode=pl.Buffered(1))
    ncp = kc.shape[2]
    return pl.pallas_call(
        functools.partial(_attn_prompt_kernel, n_cmp=n_cmp, n_blocks=n_blocks),
        grid=(batch, nch),
        in_specs=[col(d_attn), col(d_attn), col(GATE_ROWS),
                  pl.BlockSpec((tq, d), lambda b, c: (b * nch + c, 0)),
                  seq_k, seq_v, seq_k, seq_v,
                  pl.BlockSpec((1, KV_HEADS, ncp, V7X_LANES), lambda b, c: (b, 0, 0, 0)),
                  pl.BlockSpec((1, KV_HEADS, HEAD_DIM, ncp), lambda b, c: (b, 0, 0, 0)),
                  _full(wo), _full(gf)],
        out_specs=pl.BlockSpec((tq, d), lambda b, c: (b * nch + c, 0)),
        out_shape=jax.ShapeDtypeStruct((n, d), F32),
        scratch_shapes=[pltpu.VMEM((N_HEADS, V7X_LANES, tq), BF16),
                        pltpu.VMEM((KV_HEADS, HEAD_DIM, GROUP * tq), F32),
                        pltpu.VMEM((d_attn, tq), BF16)],
        compiler_params=_params(("arbitrary", "arbitrary")),
        name="attn_prompt",
    )(qt, szt, gt, h1, ksel, vts, kwin, vtw, kc, vtc, wo, gf)


HEAD_ROWS = KV_HEADS * V7X_SUBLANES


def _attn_step_cmp_kernel(ab_ref, bias_ref, w2k_ref, w2v_ref, qbd_ref, og_ref, idx_ref,
                          *, n_cmp, n_blocks, pos):
    nck = ab_ref.shape[1]
    nbp = idx_ref.shape[2] * (-(-n_blocks // idx_ref.shape[2]))
    hk, hv = _cmp_hidden(ab_ref[0], bias_ref[0])
    k = _dot(hk, w2k_ref[...]).astype(BF16)
    v = _dot(hv, w2v_ref[...]).astype(BF16)
    tok = lax.broadcasted_iota(jnp.int32, (1, nck), 1)
    mask = (CMP_STRIDE * tok + (CMP_LEN - 1) <= pos) & (tok < n_cmp)
    s = jnp.where(mask, _dot_nt(qbd_ref[0].astype(BF16), k), NEG)
    e = jnp.exp(s - jnp.max(s, axis=1, keepdims=True)) * mask.astype(F32)
    p = e * (1.0 / jnp.maximum(jnp.sum(e, axis=1, keepdims=True), 1e-30))
    r = lax.broadcasted_iota(jnp.int32, (HEAD_ROWS, 1), 0)
    p = jnp.where(r % V7X_SUBLANES < GROUP, p, 0.0)
    o = _dot(p.astype(BF16), v)
    lane = lax.broadcasted_iota(jnp.int32, (V7X_SUBLANES, KV_WIDTH), 1)
    og = jnp.zeros((V7X_SUBLANES, KV_WIDTH), F32)
    for h in range(KV_HEADS):
        og = jnp.where(lane // HEAD_DIM == h, o[h * V7X_SUBLANES:(h + 1) * V7X_SUBLANES], og)
    og_ref[0] = og
    r8 = lax.broadcasted_iota(jnp.int32, (V7X_SUBLANES, 1), 0)
    psum = jnp.zeros((V7X_SUBLANES, nck), F32)
    for h in range(KV_HEADS):
        ph = jnp.sum(p[h * V7X_SUBLANES:(h + 1) * V7X_SUBLANES], axis=0, keepdims=True)
        psum = jnp.where(r8 == h, ph, psum)
    ci = lax.broadcasted_iota(jnp.int32, (nck, 1), 0)
    blk = lax.broadcasted_iota(jnp.int32, (1, nbp), 1)
    member = _block_members(ci, blk, n_cmp, n_blocks).astype(BF16)
    imp = _dot(psum.astype(BF16), member)
    score = _block_scores(imp, blk, pos, n_blocks)
    slot = lax.broadcasted_iota(jnp.int32, idx_ref.shape[1:], 1)
    picked = jnp.full(idx_ref.shape[1:], -1, jnp.int32)
    for j, (top_s, first) in enumerate(_topk_pick(score, min(SEL_TOPK, n_blocks), 1)):
        picked = jnp.where(slot == j, jnp.where(top_s > -0.5 * FORCE, first, -1), picked)
    idx_ref[0] = picked


def _attn_step_cmp(ab, bias, cw, qbd, n_cmp, n_blocks, pos):
    ns, nck, _ = ab.shape
    return pl.pallas_call(
        functools.partial(_attn_step_cmp_kernel, n_cmp=n_cmp, n_blocks=n_blocks, pos=pos),
        grid=(ns,),
        in_specs=[pl.BlockSpec((1, nck, AB_WIDTH), lambda b: (b, 0, 0)),
                  _full(bias), _full(cw["w2k_bd"]), _full(cw["w2v_bd"]),
                  pl.BlockSpec((1,) + qbd.shape[1:], lambda b: (b, 0, 0))],
        out_specs=[pl.BlockSpec((1, V7X_SUBLANES, KV_WIDTH), lambda b: (b, 0, 0)),
                   pl.BlockSpec((1, V7X_SUBLANES, V7X_LANES), lambda b: (b, 0, 0))],
        out_shape=[jax.ShapeDtypeStruct((ns, V7X_SUBLANES, KV_WIDTH), F32),
                   jax.ShapeDtypeStruct((ns, V7X_SUBLANES, V7X_LANES), jnp.int32)],
        compiler_params=_params(("arbitrary",)),
        name="attn_step_cmp",
    )(ab, bias, cw["w2k_bd"], cw["w2v_bd"], qbd)


def _attn_step_kernel(idx_ref, pt_ref, pool_ref, q_ref, new_ref, win_ref, osel_ref, owin_ref,
                      buf, sem, *, n_past_blocks, pos, win_pos0):
    b = pl.program_id(0)
    nb = pl.num_programs(0)
    per_head = buf.shape[4] // PAGE_SIZE
    n_slots = KV_HEADS * per_head
    pages_per_seq = pt_ref.shape[0] // nb
    halves = PAGE_SIZE // SEL_BLOCK
    slot = b % 2

    def block_of(bb, h, j):
        return idx_ref[bb * n_slots + h * per_head + j]

    def copies(bb, sl):
        out = []
        for h in range(KV_HEADS):
            for j in range(per_head):
                blk = jnp.clip(block_of(bb, h, j), 0, n_past_blocks - 1)
                page = pt_ref[bb * pages_per_seq + blk // halves]
                out.append(pltpu.make_async_copy(
                    pool_ref.at[page, :, h], buf.at[sl, h, :, :, pl.ds(j * PAGE_SIZE, PAGE_SIZE)], sem.at[sl]))
        return out

    @pl.when(b == 0)
    def _():
        for cp in copies(0, 0):
            cp.start()

    @pl.when(b + 1 < nb)
    def _():
        for cp in copies(b + 1, 1 - slot):
            cp.start()

    def with_new_token(q, s, vt16, k_new, v_new, new_ok):
        s_new = jnp.where(new_ok, jnp.sum(q * k_new, axis=1, keepdims=True), NEG)
        m = jnp.maximum(jnp.max(s, axis=1, keepdims=True), s_new)
        e = jnp.exp(s - m)
        e_new = jnp.where(new_ok, jnp.exp(s_new - m), 0.0)
        den = jnp.maximum(jnp.sum(e, axis=1, keepdims=True) + e_new, 1e-30)
        return (_dot_nt(e.astype(BF16), vt16) + e_new * v_new) * (1.0 / den)

    def new_row(branch, kv, h):
        r = (branch * 2 + kv) * KV_HEADS + h
        return new_ref[0, r:r + 1, :]

    n_win = win_ref.shape[4]
    wpos = win_pos0 + lax.broadcasted_iota(jnp.int32, (1, n_win), 1)
    wmask = (wpos >= 0) & (pos - wpos >= 0) & (pos - wpos <= WINDOW)
    for h in range(KV_HEADS):
        q = q_ref[0, h]
        s = jnp.where(wmask, _dot(q.astype(BF16), win_ref[0, 0, h].astype(BF16)), NEG)
        owin_ref[0, h] = with_new_token(q, s, win_ref[0, 1, h].astype(BF16),
                                        new_row(2, 0, h), new_row(2, 1, h), True)

    for cp in copies(b, slot):
        cp.wait()

    lane = lax.broadcasted_iota(jnp.int32, (1, per_head * PAGE_SIZE), 1)
    lane_slot = lane // PAGE_SIZE
    lane_half = (lane % PAGE_SIZE) // SEL_BLOCK
    for h in range(KV_HEADS):
        code = jnp.zeros((1, per_head * PAGE_SIZE), jnp.int32)
        has_new = jnp.int32(0)
        for j in range(per_head):
            blk = block_of(b, h, j)
            past = (blk >= 0) & (blk < n_past_blocks)
            code = jnp.where(lane_slot == j, jnp.where(past, 1 + blk % halves, 0), code)
            has_new = has_new | ((blk >= n_past_blocks) & (blk * SEL_BLOCK <= pos)).astype(jnp.int32)
        q = q_ref[0, h]
        s = jnp.where(code == 1 + lane_half, _dot(q.astype(BF16), buf[slot, h, 0].astype(BF16)), NEG)
        osel_ref[0, h] = with_new_token(q, s, buf[slot, h, 1].astype(BF16),
                                        new_row(1, 0, h), new_row(1, 1, h), has_new > 0)


def _attn_step(idx, page_table, pool_t, q8, new_rows, win_t, n_past_blocks, pos, win_pos0):
    ns = q8.shape[0]
    per_head = min(SEL_TOPK, n_past_blocks + 1)
    blk = lambda a: pl.BlockSpec((1,) + a.shape[1:], lambda b, i, p: (b,) + (0,) * (a.ndim - 1))
    out = pl.BlockSpec((1, KV_HEADS, V7X_SUBLANES, HEAD_DIM), lambda b, i, p: (b, 0, 0, 0))
    out_shape = jax.ShapeDtypeStruct((ns, KV_HEADS, V7X_SUBLANES, HEAD_DIM), F32)
    return pl.pallas_call(
        functools.partial(_attn_step_kernel, n_past_blocks=n_past_blocks, pos=pos, win_pos0=win_pos0),
        grid_spec=pltpu.PrefetchScalarGridSpec(
            num_scalar_prefetch=2,
            grid=(ns,),
            in_specs=[pl.BlockSpec(memory_space=pl.ANY), blk(q8), blk(new_rows), blk(win_t)],
            out_specs=[out, out],
            scratch_shapes=[pltpu.VMEM((2, KV_HEADS, 2, HEAD_DIM, per_head * PAGE_SIZE), F32),
                            pltpu.SemaphoreType.DMA((2,))],
        ),
        out_shape=[out_shape, out_shape],
        compiler_params=_params(("arbitrary",)),
        name="attn_step",
    )(idx, page_table.reshape(-1), pool_t, q8, new_rows, win_t)


def _out_step_kernel(oc_ref, os_ref, ow_ref, g0_ref, g1_ref, g2_ref, sz_ref, h1_ref, wo_ref, gf_ref, y_ref):
    o = g0_ref[...] * oc_ref[...] + g1_ref[...] * os_ref[...] + g2_ref[...] * ow_ref[...]
    y = _dot((o * sz_ref[...]).astype(BF16), wo_ref[...])
    y_ref[...] = _rmsnorm(h1_ref[...] + y, gf_ref[...])


def _out_step(*ins):
    h1 = ins[7]
    return pl.pallas_call(
        _out_step_kernel,
        grid=(1,),
        in_specs=[_full(a) for a in ins],
        out_specs=_full(h1),
        out_shape=jax.ShapeDtypeStruct(h1.shape, F32),
        compiler_params=_params(("arbitrary",)),
        name="out_step",
    )(*ins)


def kernel(x_prompt, x_sample, cache_conv, cache_cmp_kv, cache_sel_kv, cache_win_kv, page_table,
           norm_a, conv_w_in, conv_w, conv_w_out, norm_kv, w_kv, cmp_pe, cmp_w1, cmp_w2,
           norm_b, w_qg, w_o, norm_f):
    batch, seq, d = x_prompt.shape
    dec, dec_seq, _ = x_sample.shape
    assert norm_a.shape[0] == 1 and norm_b.shape[0] == 1 and dec_seq == 1
    assert cache_conv.shape[2] == CONV_WIDTH - 1 and cache_cmp_kv.shape[1] == PAGE_SIZE
    n_pool = cache_cmp_kv.shape[0]
    n_pages = page_table.shape[1]
    past_len = n_pages * PAGE_SIZE
    dc = conv_w.shape[2]
    d_attn = N_HEADS * HEAD_DIM

    win_r, wout = _conv_weights(conv_w_in[0], conv_w_out[0])
    pw = _proj_weights(w_kv, w_qg[0])
    cw = _compress_weights(cmp_pe, cmp_w1, cmp_w2)
    g_a, g_kv, g_b, g_f = norm_a[0][None], norm_kv[None], norm_b[0][None], norm_f[None]
    wo16 = w_o[0].astype(BF16)
    bias = _compress_rows(cw["pe_rows"], cw["w1"], cw["pe_rows"].shape[1])
    to_tiles = lambda a: a.transpose(0, 2, 3, 4, 1)
    from_tiles = lambda a: a.reshape(a.shape[0], 2, KV_HEADS, HEAD_DIM, a.shape[2]).transpose(0, 4, 1, 2, 3)

    n = batch * seq
    h1_p, conv_p = _conv_prompt(x_prompt.reshape(n, d), g_a, win_r, conv_w[0], wout, seq)
    conv_p = conv_p.transpose(0, 2, 1, 3).reshape(batch, CONV_WIDTH - 1, dc)
    (kvt_c, kvt_s, kvt_w, kcmp, ksel, kwin, vts, vtw, qt, szt, gt) = _proj_prompt(
        h1_p, g_kv, g_b, pw, batch, seq)
    ab_p = _compress_rows(kcmp, cw["w1"], seq)
    kc, vtc = _cmp_tokens_prompt(ab_p, bias, cw)
    y_p = _attn_prompt(qt, szt, gt, h1_p, ksel, vts, kwin, vtw, kc, vtc, wo16, g_f, batch, seq)

    prev = cache_conv[0]
    h1_s, u_s = _conv_step(x_sample.reshape(dec, d), g_a, win_r, conv_w[0], wout, prev[:, 0], prev[:, 1])
    kv_s, q_s, sz_s, gate_s = _proj_step(h1_s, g_kv, g_b, pw)
    ab_s = _compress_paged(page_table, to_tiles(cache_cmp_kv).reshape(n_pool, 2, KV_WIDTH, PAGE_SIZE),
                           cw["w1"])
    n_cmp_s = (past_len + dec_seq) // CMP_STRIDE - 1
    assert n_cmp_s + 1 == ab_s.shape[1]
    n_blocks_s = -(-(past_len + dec_seq) // SEL_BLOCK)
    q8 = jnp.pad(q_s.reshape(dec, KV_HEADS, GROUP, HEAD_DIM),
                 ((0, 0), (0, 0), (0, V7X_SUBLANES - GROUP), (0, 0)))
    qbd = jnp.einsum("bkgd,kl->bkgld", q8, jnp.eye(KV_HEADS, dtype=F32)).reshape(dec, HEAD_ROWS, KV_WIDTH)
    ogc, idx = _attn_step_cmp(ab_s, bias, cw, qbd, n_cmp_s, n_blocks_s, past_len)
    k_sel = min(SEL_TOPK, n_blocks_s)
    idx = idx[:, :KV_HEADS, :k_sel].reshape(-1)
    n_win = cache_win_kv.shape[1]
    o_sel, o_win = _attn_step(idx, page_table, to_tiles(cache_sel_kv), q8,
                              kv_s.reshape(dec, N_BRANCH * 2 * KV_HEADS, HEAD_DIM), to_tiles(cache_win_kv),
                              past_len // SEL_BLOCK, past_len, past_len - n_win)
    heads = lambda o: o[:, :, :GROUP].reshape(dec, d_attn)
    o_cmp = ogc[:, :GROUP].reshape(dec, GROUP, KV_HEADS, HEAD_DIM).transpose(0, 2, 1, 3).reshape(dec, d_attn)
    gates = gate_s[:, :N_BRANCH * N_HEADS].reshape(dec, N_HEADS, N_BRANCH)
    g_br = [jnp.repeat(gates[:, :, br], HEAD_DIM, axis=1) for br in range(N_BRANCH)]
    y_s = _out_step(o_cmp, heads(o_sel), heads(o_win), g_br[0], g_br[1], g_br[2], sz_s, h1_s, wo16, g_f)

    kv5 = lambda a, lead: a.reshape(lead + (2, KV_HEADS, HEAD_DIM))
    keep_p = min(WINDOW, seq)
    kvs3 = kv_s.reshape(dec, N_BRANCH, ROW_WIDTH)
    win_all = jnp.concatenate([cache_win_kv, kv5(kvs3[:, 2], (dec, 1))], axis=1)
    keep_s = min(WINDOW, win_all.shape[1])
    conv_s = jnp.stack([prev[:, 1], u_s], axis=1)[None]
    return (y_p.reshape(batch, seq, d), y_s.reshape(dec, 1, d), conv_p[None], conv_s,
            from_tiles(kvt_c), kv5(kvs3[:, 0], (dec, 1)),
            from_tiles(kvt_s), kv5(kvs3[:, 1], (dec, 1)),
            from_tiles(kvt_w[:, :, seq - keep_p:]), win_all[:, win_all.shape[1] - keep_s:])
```

```python
import functools
import math

import jax
import jax.numpy as jnp
from jax import lax
from jax.experimental import pallas as pl
from jax.experimental.pallas import tpu as pltpu

F32 = jnp.float32
BF16 = jnp.bfloat16

N_HEADS = 16
HEAD_DIM = 64
KV_HEADS = 4
GROUP = N_HEADS // KV_HEADS
KV_WIDTH = KV_HEADS * HEAD_DIM
ROW_WIDTH = 2 * KV_WIDTH
N_BRANCH = 3
CONV_WIDTH = 3
CMP_STRIDE = 16
CMP_LEN = 2 * CMP_STRIDE
SEL_BLOCK = 64
SEL_TOPK = 16
WINDOW = 512
PAGE_SIZE = 128
EPS = 1e-6
NEG = -1e30
FORCE = 1e9
SCALE = HEAD_DIM ** -0.5
LOG2E = math.log2(math.e)

V7X_LANES = 128
V7X_SUBLANES = 8
V7X_BF16_ROWS = 16
V7X_VMEM_LIMIT_BYTES = 56 * 1024 * 1024

CONV_CHUNK = 256
TOKEN_TILE_CONV = 512
TOKEN_TILE_PROJ = 256
Q_TILE = 256
CMP_PAGES_PER_STEP = 32
REGROUP_UNROLL = 4
QK_AHEAD = 4
GATE_ROWS = 128
V_ROWS = HEAD_DIM + V7X_BF16_ROWS

_NT = (((1,), (1,)), ((), ()))
_TN = (((0,), (0,)), ((), ()))


def _params(sem):
    return pltpu.CompilerParams(dimension_semantics=sem, vmem_limit_bytes=V7X_VMEM_LIMIT_BYTES)


def _rmsnorm(x, g):
    return x * lax.rsqrt(jnp.mean(x * x, axis=-1, keepdims=True) + EPS) * g


def _sigmoid(x):
    return 1.0 / (1.0 + jnp.exp(-x))


def _silu(x):
    return x * _sigmoid(x)


def _dot(a, b):
    return jnp.dot(a, b, preferred_element_type=F32)


def _dot_nt(a, b):
    return lax.dot_general(a, b, _NT, preferred_element_type=F32)


def _full(a):
    return pl.BlockSpec(a.shape, lambda *_: (0,) * a.ndim)


def _conv_prompt_kernel(x_ref, g_ref, win_ref, cw_ref, wout_ref, h1_ref, st_ref, ubuf_s, carry_s,
                        *, tiles_per_seq):
    i = pl.program_id(0)
    t = x_ref.shape[0]
    dc = cw_ref.shape[1]
    cw = CONV_CHUNK
    nc = dc // cw
    lo = V7X_SUBLANES
    x = x_ref[...]
    xn = _rmsnorm(x, g_ref[...]).astype(BF16)

    @pl.when(i % tiles_per_seq == 0)
    def _():
        carry_s[...] = jnp.zeros_like(carry_s)

    def in_proj(j):
        return [_dot(xn, win_ref[:, k * dc + j * cw:k * dc + (j + 1) * cw]) for k in range(4)]

    def gate(j, proj):
        bg, cg, hh, z = proj
        u = cg * hh
        ubuf_s[j, 0:lo, :] = carry_s[j]
        ubuf_s[j, lo:lo + t, :] = u
        w = cw_ref[:, j * cw:(j + 1) * cw]
        v = w[0:1] * ubuf_s[j, lo - 2:lo - 2 + t, :] + w[1:2] * ubuf_s[j, lo - 1:lo - 1 + t, :] + w[2:3] * u
        carry_s[j] = u[t - lo:t]
        st_ref[0, j] = ubuf_s[j, lo + t - 2:lo + t, :]
        return (_silu(z) * bg * v).astype(BF16)

    proj = in_proj(0)
    y = None
    for j in range(nc):
        nxt = in_proj(j + 1) if j + 1 < nc else None
        d = _dot(gate(j, proj), wout_ref[j * cw:(j + 1) * cw, :])
        y = d if y is None else y + d
        proj = nxt
    h1_ref[...] = x + y


def _conv_step_kernel(x_ref, g_ref, wb_ref, wc_ref, wh_ref, wz_ref, cw_ref, wout_ref, p0_ref, p1_ref,
                      h1_ref, u_ref, xn_s, acc_s):
    j = pl.program_id(0)

    @pl.when(j == 0)
    def _():
        xn_s[...] = _rmsnorm(x_ref[...], g_ref[...]).astype(BF16)
        acc_s[...] = jnp.zeros_like(acc_s)

    xn = xn_s[...]
    bg, z = _dot(xn, wb_ref[...]), _dot(xn, wz_ref[...])
    u = _dot(xn, wc_ref[...]) * _dot(xn, wh_ref[...])
    w = cw_ref[...]
    v = w[0:1] * p0_ref[...] + w[1:2] * p1_ref[...] + w[2:3] * u
    u_ref[...] = u
    gated = _silu(z) * bg * v
    acc_s[...] += _dot(gated.astype(BF16), wout_ref[...])

    @pl.when(j == pl.num_programs(0) - 1)
    def _():
        h1_ref[...] = x_ref[...] + acc_s[...]


def _conv_prompt(x, g, win, conv_w, wout, seq):
    n, d = x.shape
    nc = conv_w.shape[1] // CONV_CHUNK
    t = min(TOKEN_TILE_CONV, seq)
    assert seq % t == 0 and t % V7X_SUBLANES == 0
    tiles_per_seq = seq // t
    resident = lambda a: pl.BlockSpec(a.shape, lambda i: (0,) * a.ndim, pipeline_mode=pl.Buffered(1))
    return pl.pallas_call(
        functools.partial(_conv_prompt_kernel, tiles_per_seq=tiles_per_seq),
        grid=(n // t,),
        in_specs=[
            pl.BlockSpec((t, d), lambda i: (i, 0)),
            _full(g), resident(win), _full(conv_w), resident(wout),
        ],
        out_specs=[
            pl.BlockSpec((t, d), lambda i: (i, 0)),
            pl.BlockSpec((1, nc, CONV_WIDTH - 1, CONV_CHUNK), lambda i: (i // tiles_per_seq, 0, 0, 0)),
        ],
        out_shape=[
            jax.ShapeDtypeStruct((n, d), F32),
            jax.ShapeDtypeStruct((n // seq, nc, CONV_WIDTH - 1, CONV_CHUNK), F32),
        ],
        scratch_shapes=[
            pltpu.VMEM((nc, t + V7X_SUBLANES, CONV_CHUNK), F32),
            pltpu.VMEM((nc, V7X_SUBLANES, CONV_CHUNK), F32),
        ],
        compiler_params=_params(("arbitrary",)),
        name="conv_prompt",
    )(x, g, win, conv_w, wout)


def _conv_step(x, g, win, conv_w, wout, prev0, prev1):
    n, d = x.shape
    dc = conv_w.shape[1]
    nc = dc // CONV_CHUNK
    split = lambda k: pl.BlockSpec((d, CONV_CHUNK), lambda j: (0, k * nc + j))
    return pl.pallas_call(
        _conv_step_kernel,
        grid=(nc,),
        in_specs=[
            pl.BlockSpec((n, d), lambda j: (0, 0)),
            pl.BlockSpec((1, d), lambda j: (0, 0)),
            split(0), split(1), split(2), split(3),
            pl.BlockSpec((CONV_WIDTH, CONV_CHUNK), lambda j: (0, j)),
            pl.BlockSpec((CONV_CHUNK, d), lambda j: (j, 0)),
            pl.BlockSpec((n, CONV_CHUNK), lambda j: (0, j)),
            pl.BlockSpec((n, CONV_CHUNK), lambda j: (0, j)),
        ],
        out_specs=[
            pl.BlockSpec((n, d), lambda j: (0, 0)),
            pl.BlockSpec((n, CONV_CHUNK), lambda j: (0, j)),
        ],
        out_shape=[jax.ShapeDtypeStruct((n, d), F32), jax.ShapeDtypeStruct((n, dc), F32)],
        scratch_shapes=[pltpu.VMEM((n, d), BF16), pltpu.VMEM((n, d), F32)],
        compiler_params=_params(("arbitrary",)),
        name="conv_step",
    )(x, g, win, win, win, win, conv_w, wout, prev0, prev1)


def _proj_prompt_kernel(h_ref, gkv_ref, gq_ref, wrow_ref, wkvt_ref, wqzt_ref,
                        kvtc_ref, kvts_ref, kvtw_ref, kcmp_ref, ksel_ref, kwin_ref, vts_ref, vtw_ref,
                        qt_ref, szt_ref, gt_ref, *, tiles_per_seq):
    i = pl.program_id(0)
    t = h_ref.shape[0]
    d_attn = N_HEADS * HEAD_DIM
    h = h_ref[...]
    kvn = _rmsnorm(h, gkv_ref[...]).astype(BF16)
    kvt = _dot_nt(wkvt_ref[...], kvn)
    kvtc_ref[0] = kvt[0:ROW_WIDTH]
    kvts_ref[0] = kvt[ROW_WIDTH:2 * ROW_WIDTH]
    kvtw_ref[0] = kvt[2 * ROW_WIDTH:3 * ROW_WIDTH]
    ones = jnp.ones((V_ROWS - HEAD_DIM, t), BF16)
    for k in range(KV_HEADS):
        lo = ROW_WIDTH + KV_WIDTH + k * HEAD_DIM
        vts_ref[0, k, 0:HEAD_DIM] = kvt[lo:lo + HEAD_DIM].astype(BF16)
        vts_ref[0, k, HEAD_DIM:V_ROWS] = ones
        lo = 2 * ROW_WIDTH + KV_WIDTH + k * HEAD_DIM
        vtw_ref[0, k, 0:HEAD_DIM] = kvt[lo:lo + HEAD_DIM].astype(BF16)
        vtw_ref[0, k, HEAD_DIM:V_ROWS] = ones
    row = _dot(kvn, wrow_ref[...])
    for grp in range(ROW_WIDTH // V7X_LANES):
        kcmp_ref[grp] = row[:, grp * V7X_LANES:(grp + 1) * V7X_LANES]
    pos = (i % tiles_per_seq) * t + lax.broadcasted_iota(jnp.int32, (t, V7X_LANES), 0)
    lane = lax.broadcasted_iota(jnp.int32, (t, V7X_LANES), 1)
    onehot = (lane - HEAD_DIM == pos // SEL_BLOCK).astype(F32)
    for k in range(KV_HEADS):
        lo = ROW_WIDTH + k * V7X_LANES
        ksel_ref[0, k] = (row[:, lo:lo + V7X_LANES] + onehot).astype(BF16)
        lo = ROW_WIDTH + (KV_HEADS + k) * V7X_LANES
        kwin_ref[0, k] = row[:, lo:lo + V7X_LANES].astype(BF16)
    qn = _rmsnorm(h, gq_ref[...]).astype(BF16)
    qz = _dot_nt(wqzt_ref[...], qn)
    qt_ref[...] = (qz[0:d_attn] * (SCALE * LOG2E)).astype(BF16)
    szt_ref[...] = _silu(qz[d_attn:2 * d_attn]).astype(BF16)
    gt_ref[...] = _sigmoid(qz[2 * d_attn:2 * d_attn + GATE_ROWS])


def _proj_step_kernel(h_ref, gkv_ref, gq_ref, wkvt_ref, wqzt_ref, kv_ref, q_ref, sz_ref, g_ref):
    d_attn = N_HEADS * HEAD_DIM
    h = h_ref[...]
    kvn = _rmsnorm(h, gkv_ref[...]).astype(BF16)
    kv_ref[...] = _dot_nt(kvn, wkvt_ref[...])
    qn = _rmsnorm(h, gq_ref[...]).astype(BF16)
    qz = _dot_nt(qn, wqzt_ref[...])
    q_ref[...] = qz[:, 0:d_attn] * SCALE
    sz_ref[...] = _silu(qz[:, d_attn:2 * d_attn])
    g_ref[...] = _sigmoid(qz[:, 2 * d_attn:2 * d_attn + GATE_ROWS])


def _proj_weights(w_kv, w_qg):
    d = w_kv.shape[0]
    d_attn = N_HEADS * HEAD_DIM
    n_gate = N_BRANCH * N_HEADS
    wkv4 = w_kv.reshape(d, N_BRANCH, 2, KV_HEADS, HEAD_DIM)
    pad = jnp.zeros((d, KV_HEADS, V7X_LANES - HEAD_DIM), w_kv.dtype)
    ksel = jnp.concatenate([wkv4[:, 1, 0], pad], axis=2).reshape(d, KV_HEADS * V7X_LANES)
    kwin = jnp.concatenate([wkv4[:, 2, 0], pad], axis=2).reshape(d, KV_HEADS * V7X_LANES)
    wrow = jnp.concatenate([w_kv[:, 0:ROW_WIDTH], ksel, kwin], axis=1).astype(BF16)
    w_q = w_qg[:, :d_attn]
    w_g = w_qg[:, d_attn:d_attn + n_gate]
    w_z = w_qg[:, d_attn + n_gate:]
    gpad = jnp.zeros((d, GATE_ROWS - n_gate), w_qg.dtype)
    wqzt = jnp.concatenate([w_q, w_z, w_g, gpad], axis=1).T.astype(BF16)
    return dict(wrow=wrow, wkvt=w_kv.T.astype(BF16), wqzt=wqzt)


def _proj_prompt(h1, gkv, gq, pw, batch, seq):
    n, d = h1.shape
    t = TOKEN_TILE_PROJ
    assert seq % t == 0 and t == Q_TILE and seq // SEL_BLOCK <= V7X_LANES - HEAD_DIM
    d_attn = N_HEADS * HEAD_DIM
    nt = n // t
    tps = seq // t
    tok = lambda w: pl.BlockSpec((t, w), lambda i: (i, 0))
    kvt = pl.BlockSpec((1, ROW_WIDTH, t), lambda i: (i // tps, 0, i % tps))
    ktile = pl.BlockSpec((1, KV_HEADS, t, V7X_LANES), lambda i: (i, 0, 0, 0))
    vtile = pl.BlockSpec((1, KV_HEADS, V_ROWS, t), lambda i: (i, 0, 0, 0))
    col = lambda r: pl.BlockSpec((r, t), lambda i: (0, i))
    kvt_shape = jax.ShapeDtypeStruct((batch, ROW_WIDTH, seq), F32)
    return pl.pallas_call(
        functools.partial(_proj_prompt_kernel, tiles_per_seq=tps),
        grid=(nt,),
        in_specs=[tok(d), _full(gkv), _full(gq), _full(pw["wrow"]), _full(pw["wkvt"]), _full(pw["wqzt"])],
        out_specs=[kvt, kvt, kvt, pl.BlockSpec((ROW_WIDTH // V7X_LANES, t, V7X_LANES), lambda i: (0, i, 0)),
                   ktile, ktile, vtile, vtile,
                   col(d_attn), col(d_attn), col(GATE_ROWS)],
        out_shape=[
            kvt_shape, kvt_shape, kvt_shape,
            jax.ShapeDtypeStruct((ROW_WIDTH // V7X_LANES, n, V7X_LANES), F32),
            jax.ShapeDtypeStruct((nt, KV_HEADS, t, V7X_LANES), BF16),
            jax.ShapeDtypeStruct((nt, KV_HEADS, t, V7X_LANES), BF16),
            jax.ShapeDtypeStruct((nt, KV_HEADS, V_ROWS, t), BF16),
            jax.ShapeDtypeStruct((nt, KV_HEADS, V_ROWS, t), BF16),
            jax.ShapeDtypeStruct((d_attn, n), BF16),
            jax.ShapeDtypeStruct((d_attn, n), BF16),
            jax.ShapeDtypeStruct((GATE_ROWS, n), F32),
        ],
        compiler_params=_params(("arbitrary",)),
        name="proj_prompt",
    )(h1, gkv, gq, pw["wrow"], pw["wkvt"], pw["wqzt"])


def _proj_step(h1, gkv, gq, pw):
    n, d = h1.shape
    d_attn = N_HEADS * HEAD_DIM
    ins = [h1, gkv, gq, pw["wkvt"], pw["wqzt"]]
    outs = [N_BRANCH * ROW_WIDTH, d_attn, d_attn, GATE_ROWS]
    return pl.pallas_call(
        _proj_step_kernel,
        grid=(1,),
        in_specs=[_full(a) for a in ins],
        out_specs=[pl.BlockSpec((n, w), lambda i: (0, 0)) for w in outs],
        out_shape=[jax.ShapeDtypeStruct((n, w), F32) for w in outs],
        compiler_params=_params(("arbitrary",)),
        name="proj_step",
    )(*ins)


CHUNK_FEATS = CMP_STRIDE * ROW_WIDTH
AB_WIDTH = 2 * 2 * 2 * V7X_LANES


def _compress_cols(get_cols, w_ref):
    outs = []
    for c in range(2):
        for kp in range(KV_HEADS // 2):
            base = c * KV_WIDTH + kp * V7X_LANES
            lhs = jnp.concatenate([get_cols(r, base).astype(BF16) for r in range(CMP_STRIDE)], axis=1)
            outs.append(_dot(lhs, w_ref[c]))
    return jnp.concatenate(outs, axis=1)


def _compress_rows_kernel(x_ref, w_ref, ab_ref):
    m = x_ref.shape[1] // CMP_STRIDE
    ab_ref[0] = _compress_cols(
        lambda r, lo: x_ref[lo // V7X_LANES, pl.ds(r, m, stride=CMP_STRIDE), :], w_ref)


def _compress_paged_kernel(pt_ref, pool_ref, perm_ref, w_ref, ab_ref, buf, lhs_s, sem):
    step = pl.program_id(0) * pl.num_programs(1) + pl.program_id(1)
    n_steps = pl.num_programs(0) * pl.num_programs(1)
    pages = buf.shape[1]
    slot = step % 2

    def copies(s, sl):
        return [pltpu.make_async_copy(pool_ref.at[pt_ref[s * pages + p]], buf.at[sl, p], sem.at[sl])
                for p in range(pages)]

    @pl.when(step == 0)
    def _():
        for cp in copies(0, 0):
            cp.start()

    @pl.when(step + 1 < n_steps)
    def _():
        for cp in copies(step + 1, 1 - slot):
            cp.start()

    for cp in copies(step, slot):
        cp.wait()

    pair_chunks = 2 * (PAGE_SIZE // CMP_STRIDE)
    unroll = math.gcd(pages // 2, REGROUP_UNROLL)

    def regroup(it, carry):
        tiles = []
        for u in range(unroll):
            pp = it * unroll + u
            for c in range(2):
                xt = jnp.concatenate([buf[slot, 2 * pp, c], buf[slot, 2 * pp + 1, c]], axis=1).astype(BF16)
                tiles.append((pp, c, _dot_nt(perm_ref[...], xt)))
        for pp, c, t in tiles:
            t = t.astype(BF16)
            m0 = pl.multiple_of(pp * pair_chunks, pair_chunks)
            for r in range(CMP_STRIDE):
                for kp in range(KV_HEADS // 2):
                    lhs_s[c * (KV_HEADS // 2) + kp, pl.ds(m0, pair_chunks), r * V7X_LANES:(r + 1) * V7X_LANES] = (
                        t[r * pair_chunks:(r + 1) * pair_chunks, kp * V7X_LANES:(kp + 1) * V7X_LANES])
        return carry

    lax.fori_loop(0, pages // 2 // unroll, regroup, 0)
    ab_ref[0] = jnp.concatenate(
        [_dot(lhs_s[g], w_ref[g // (KV_HEADS // 2)]) for g in range(lhs_s.shape[0])], axis=1)


def _compress_weights(cmp_pe, cmp_w1, cmp_w2):
    hid = cmp_w1.shape[2]
    w1r = cmp_w1.reshape(2, 2, CMP_STRIDE, HEAD_DIM, hid)
    eye = jnp.eye(2, dtype=cmp_w1.dtype)
    w = jnp.einsum("cardE,kl->crkdalE", w1r, eye)
    w = w.reshape(2, CMP_STRIDE * 2 * HEAD_DIM, 2 * 2 * hid).astype(BF16)
    pe = cmp_pe.reshape(2, 2, CMP_STRIDE, HEAD_DIM).transpose(1, 2, 0, 3)
    pe = jnp.broadcast_to(pe[:, :, :, None, :], (2, CMP_STRIDE, 2, KV_HEADS, HEAD_DIM))
    pe_rows = jnp.zeros((V7X_SUBLANES, CHUNK_FEATS), F32).at[0:2].set(pe.reshape(2, CHUNK_FEATS))
    pe_rows = pe_rows.reshape(V7X_SUBLANES * CMP_STRIDE, ROW_WIDTH // V7X_LANES, V7X_LANES).transpose(1, 0, 2)
    eye4 = jnp.eye(KV_HEADS, dtype=cmp_w2.dtype)
    w2k = jnp.einsum("ed,kl->keld", cmp_w2[0], eye4)
    w2k_pad = jnp.concatenate([w2k, jnp.zeros_like(w2k)], axis=3)
    w2k_pad = w2k_pad.reshape(KV_HEADS * hid, KV_HEADS * V7X_LANES).astype(BF16)
    w2k_bd = w2k.reshape(KV_HEADS * hid, KV_WIDTH).astype(BF16)
    w2v_bd = jnp.einsum("ed,kl->keld", cmp_w2[1], eye4).reshape(KV_HEADS * hid, KV_WIDTH).astype(BF16)
    return dict(w1=w, pe_rows=pe_rows, w2k_pad=w2k_pad, w2k_bd=w2k_bd, w2v_bd=w2v_bd,
                w2vt=w2v_bd.T)


def _compress_rows(x, w1, seq):
    groups, n, lanes = x.shape
    ns = n // seq
    nck = seq // CMP_STRIDE
    return pl.pallas_call(
        _compress_rows_kernel,
        grid=(ns,),
        in_specs=[pl.BlockSpec((groups, seq, lanes), lambda b: (0, b, 0)), _full(w1)],
        out_specs=pl.BlockSpec((1, nck, AB_WIDTH), lambda b: (b, 0, 0)),
        out_shape=jax.ShapeDtypeStruct((ns, nck, AB_WIDTH), F32),
        compiler_params=_params(("arbitrary",)),
        name="compress_rows",
    )(x, w1)


def _compress_paged(page_table, pool, w1):
    ns, n_pages = page_table.shape
    pages = min(CMP_PAGES_PER_STEP, n_pages)
    assert n_pages % pages == 0 and pages % 2 == 0
    assert pool.shape[1:] == (2, KV_WIDTH, PAGE_SIZE) and PAGE_SIZE == V7X_LANES
    steps = n_pages // pages
    chunks_per_page = PAGE_SIZE // CMP_STRIDE
    assert 2 * chunks_per_page == V7X_BF16_ROWS
    m = pages * chunks_per_page
    t = jnp.arange(2 * PAGE_SIZE)
    r, ch = t // (2 * chunks_per_page), t % (2 * chunks_per_page)
    src = (ch // chunks_per_page) * PAGE_SIZE + (ch % chunks_per_page) * CMP_STRIDE + r
    perm = (t[None, :] == src[:, None]).astype(BF16)
    return pl.pallas_call(
        _compress_paged_kernel,
        grid_spec=pltpu.PrefetchScalarGridSpec(
            num_scalar_prefetch=1,
            grid=(ns, steps),
            in_specs=[pl.BlockSpec(memory_space=pl.ANY),
                      pl.BlockSpec(perm.shape, lambda b, s, pt: (0, 0)),
                      pl.BlockSpec(w1.shape, lambda b, s, pt: (0, 0, 0))],
            out_specs=pl.BlockSpec((1, m, AB_WIDTH), lambda b, s, pt: (b, s, 0)),
            scratch_shapes=[pltpu.VMEM((2, pages) + pool.shape[1:], F32),
                            pltpu.VMEM((ROW_WIDTH // V7X_LANES, m, CMP_STRIDE * V7X_LANES), BF16),
                            pltpu.SemaphoreType.DMA((2,))],
        ),
        out_shape=jax.ShapeDtypeStruct((ns, n_pages * chunks_per_page, AB_WIDTH), F32),
        compiler_params=_params(("arbitrary", "arbitrary")),
        name="compress_paged",
    )(page_table.reshape(-1), pool, perm, w1)


def _cmp_hidden(ab, bias):
    n = ab.shape[0]
    hid = []
    for grp in range(4):
        base = grp * 2 * V7X_LANES
        a = ab[:, base:base + V7X_LANES] + bias[0:1, base:base + V7X_LANES]
        b = ab[:, base + V7X_LANES:base + 2 * V7X_LANES] + bias[1:2, base + V7X_LANES:base + 2 * V7X_LANES]
        hid.append(_silu(a + pltpu.roll(b, n - 1, 0)))
    return (jnp.concatenate(hid[0:2], axis=1).astype(BF16),
            jnp.concatenate(hid[2:4], axis=1).astype(BF16))


def _cmp_tokens_prompt_kernel(ab_ref, bias_ref, w2k_ref, w2vt_ref, kc_ref, vtc_ref):
    hk, hv = _cmp_hidden(ab_ref[0], bias_ref[0])
    k = _dot(hk, w2k_ref[...]).astype(BF16)
    vt = _dot_nt(w2vt_ref[...], hv).astype(BF16)
    for h in range(KV_HEADS):
        kc_ref[0, h] = k[:, h * V7X_LANES:(h + 1) * V7X_LANES]
        vtc_ref[0, h] = vt[h * HEAD_DIM:(h + 1) * HEAD_DIM]


def _cmp_tokens_prompt(ab, bias, cw):
    ns, nck, _ = ab.shape
    return pl.pallas_call(
        _cmp_tokens_prompt_kernel,
        grid=(ns,),
        in_specs=[pl.BlockSpec((1, nck, AB_WIDTH), lambda b: (b, 0, 0)),
                  _full(bias), _full(cw["w2k_pad"]), _full(cw["w2vt"])],
        out_specs=[pl.BlockSpec((1, KV_HEADS, nck, V7X_LANES), lambda b: (b, 0, 0, 0)),
                   pl.BlockSpec((1, KV_HEADS, HEAD_DIM, nck), lambda b: (b, 0, 0, 0))],
        out_shape=[jax.ShapeDtypeStruct((ns, KV_HEADS, nck, V7X_LANES), BF16),
                   jax.ShapeDtypeStruct((ns, KV_HEADS, HEAD_DIM, nck), BF16)],
        compiler_params=_params(("arbitrary",)),
        name="cmp_tokens_prompt",
    )(ab, bias, cw["w2k_pad"], cw["w2vt"])


def _topk_pick(score, k, axis):
    n = score.shape[axis]
    idx = lax.broadcasted_iota(jnp.int32, score.shape, axis).astype(F32)
    s = score
    for _ in range(k):
        m = jnp.max(s, axis=axis, keepdims=True)
        first = jnp.min(jnp.where(s == m, idx, float(n)), axis=axis, keepdims=True)
        s = jnp.where(idx == first, -jnp.inf, s)
        yield m, first.astype(jnp.int32)


def _block_scores(imp, blk, pos, n_blocks):
    valid = blk * SEL_BLOCK <= pos
    cur = pos // SEL_BLOCK
    forced = (blk == 0) | (blk == cur) | (blk == cur - 1)
    score = jnp.where(valid & forced, FORCE, jnp.where(valid, imp, -FORCE))
    return jnp.where(blk < n_blocks, score, -jnp.inf)


def _block_members(ci, blk, n_cmp, n_blocks):
    ratio = SEL_BLOCK // CMP_STRIDE
    return ((ci >= ratio * blk - (CMP_LEN // CMP_STRIDE - 1)) & (ci <= ratio * blk + ratio - 1)
            & (ci < n_cmp) & (blk < n_blocks))


def _softmax_tiles(m_ref, acc_ref, scores, vts):
    m = m_ref[...]
    m_new = m
    for s in scores:
        m_new = jnp.maximum(m_new, jnp.max(s, axis=0, keepdims=True))
    m16 = m_new.astype(BF16)
    m_new = m16.astype(F32)
    acc = jnp.exp2(m - m_new) * acc_ref[...]
    for s, vt in zip(scores, vts):
        acc = acc + _dot(vt, jnp.exp2(s.astype(BF16) - m16))
    m_ref[...] = m_new
    acc_ref[...] = acc


def _softmax_finish(acc_ref):
    return acc_ref[0:HEAD_DIM, :] * (1.0 / jnp.maximum(acc_ref[HEAD_DIM:HEAD_DIM + 1, :], 1e-30))


def _attn_prompt_kernel(qt_ref, szt_ref, gt_ref, h1_ref, ksel_ref, vts_ref, kwin_ref, vtw_ref,
                        kc_ref, vtc_ref, wo_ref, gf_ref, y_ref, qa_s, ocmp_s, osz_s, m_s, acc_s, comb_s,
                        *, n_cmp, n_blocks):
    c = pl.program_id(1)
    tq = qt_ref.shape[1]
    ncp = kc_ref.shape[2]
    nbp = V7X_LANES - HEAD_DIM
    pos = c * tq + lax.broadcasted_iota(jnp.int32, (1, tq), 1)
    pos_rows = jnp.concatenate([pos] * GROUP, axis=1)

    kj = lax.broadcasted_iota(jnp.int32, (tq, tq), 0)
    qi = lax.broadcasted_iota(jnp.int32, (tq, tq), 1)
    causal = jnp.where(kj <= qi, 0.0, NEG)
    lower = jnp.where(kj >= qi, jnp.where(c >= 2, 0.0, NEG), NEG)
    has1 = jnp.where(c >= 1, 0.0, NEG)

    tok = lax.broadcasted_iota(jnp.int32, (ncp, 1), 0)
    cmp_mask = (CMP_STRIDE * tok + (CMP_LEN - 1) <= pos_rows) & (tok < n_cmp)
    blk = lax.broadcasted_iota(jnp.int32, (nbp, 1), 0)
    ci = lax.broadcasted_iota(jnp.int32, (1, ncp), 1)
    member_t = _block_members(ci, blk, n_cmp, n_blocks).astype(BF16)
    k_top = min(SEL_TOPK, n_blocks)

    for kvh in range(KV_HEADS):
        q_h = [qt_ref[(kvh * GROUP + g) * HEAD_DIM:(kvh * GROUP + g + 1) * HEAD_DIM, :] for g in range(GROUP)]
        q_t = jnp.concatenate(q_h, axis=1)
        qa0 = jnp.concatenate([q_t, jnp.zeros_like(q_t)], axis=0)
        sc = jnp.where(cmp_mask, _dot(kc_ref[0, kvh], qa0), NEG)
        e = jnp.exp2(sc - jnp.max(sc, axis=0, keepdims=True)) * cmp_mask.astype(F32)
        pc = e * (1.0 / jnp.maximum(jnp.sum(e, axis=0, keepdims=True), 1e-30))
        ocmp_s[kvh] = _dot(vtc_ref[0, kvh], pc.astype(BF16))
        psum = pc[:, 0:tq]
        for g in range(1, GROUP):
            psum = psum + pc[:, g * tq:(g + 1) * tq]
        imp = _dot(member_t, psum.astype(BF16))
        score = _block_scores(imp, blk, pos, n_blocks)
        chosen = jnp.zeros((nbp, tq), jnp.int32)
        for top_s, first in _topk_pick(score, k_top, 0):
            chosen = jnp.where((blk == first) & (top_s > -0.5 * FORCE), 1, chosen)
        mask_t = jnp.where(chosen > 0, 0.0, NEG).astype(BF16)
        for g in range(GROUP):
            qa_s[kvh * GROUP + g] = jnp.concatenate([q_h[g], mask_t], axis=0)

    def tiles(k_ref, v_ref, kts, biases):
        keys = [[k_ref[kt, kvh] for kt in kts] for kvh in range(KV_HEADS)]
        vts = [[v_ref[kt, kvh] for kt in kts] for kvh in range(KV_HEADS)]

        def qk(h):
            qa = qa_s[h]
            return [_dot(k, qa) if b is None else _dot(k, qa) + b for k, b in zip(keys[h // GROUP], biases)]

        scores = {h: qk(h) for h in range(QK_AHEAD)}
        for h in range(N_HEADS):
            _softmax_tiles(m_s.at[h], acc_s.at[h], scores.pop(h), vts[h // GROUP])
            if h + QK_AHEAD < N_HEADS:
                scores[h + QK_AHEAD] = qk(h + QK_AHEAD)

    def reset():
        m_s[...] = jnp.full(m_s.shape, NEG, F32)
        acc_s[...] = jnp.zeros(acc_s.shape, F32)

    gate = lambda h, br: gt_ref[h * N_BRANCH + br:h * N_BRANCH + br + 1, :]
    reset()

    def sel_pair(j, carry):
        tiles(ksel_ref, vts_ref, (2 * j, 2 * j + 1), (None, None))
        return carry

    lax.fori_loop(0, c // 2, sel_pair, 0)
    odd = jnp.where(c % 2 == 1, 0.0, NEG)
    tiles(ksel_ref, vts_ref, (jnp.maximum(c - 1, 0), c), (odd, causal))
    for h in range(N_HEADS):
        kvh, g = divmod(h, GROUP)
        comb_s[h] = gate(h, 0) * ocmp_s[kvh, :, g * tq:(g + 1) * tq] + gate(h, 1) * _softmax_finish(acc_s.at[h])
    reset()
    tiles(kwin_ref, vtw_ref, (jnp.maximum(c - 2, 0), jnp.maximum(c - 1, 0), c), (lower, has1, causal))
    for h in range(N_HEADS):
        o = comb_s[h] + gate(h, 2) * _softmax_finish(acc_s.at[h])
        rows = slice(h * HEAD_DIM, (h + 1) * HEAD_DIM)
        osz_s[rows, :] = (o * szt_ref[rows, :].astype(F32)).astype(BF16)
    y = lax.dot_general(osz_s[...], wo_ref[...], _TN, preferred_element_type=F32)
    y_ref[...] = _rmsnorm(h1_ref[...] + y, gf_ref[...])


def _attn_prompt(qt, szt, gt, h1, ksel, vts, kwin, vtw, kc, vtc, wo, gf, batch, seq):
    n, d = h1.shape
    tq = Q_TILE
    nch = seq // tq
    d_attn = N_HEADS * HEAD_DIM
    n_cmp = seq // CMP_STRIDE - 1
    n_blocks = -(-seq // SEL_BLOCK)
    col = lambda r: pl.BlockSpec((r, tq), lambda b, c: (0, b * nch + c))
    seq_k = pl.BlockSpec((nch, KV_HEADS, tq, V7X_LANES), lambda b, c: (b, 0, 0, 0),
                         pipeline_mode=pl.Buffered(1))
    seq_v = pl.BlockSpec((nch, KV_HEADS, V_ROWS, tq), lambda b, c: (b, 0, 0, 0),
                         pipeline_mode=pl.Buffered(1))
    ncp = kc.shape[2]
    return pl.pallas_call(
        functools.partial(_attn_prompt_kernel, n_cmp=n_cmp, n_blocks=n_blocks),
        grid=(batch, nch),
        in_specs=[col(d_attn), col(d_attn), col(GATE_ROWS),
                  pl.BlockSpec((tq, d), lambda b, c: (b * nch + c, 0)),
                  seq_k, seq_v, seq_k, seq_v,
                  pl.BlockSpec((1, KV_HEADS, ncp, V7X_LANES), lambda b, c: (b, 0, 0, 0)),
                  pl.BlockSpec((1, KV_HEADS, HEAD_DIM, ncp), lambda b, c: (b, 0, 0, 0)),
                  _full(wo), _full(gf)],
        out_specs=pl.BlockSpec((tq, d), lambda b, c: (b * nch + c, 0)),
        out_shape=jax.ShapeDtypeStruct((n, d), F32),
        scratch_shapes=[pltpu.VMEM((N_HEADS, V7X_LANES, tq), BF16),
                        pltpu.VMEM((KV_HEADS, HEAD_DIM, GROUP * tq), F32),
                        pltpu.VMEM((d_attn, tq), BF16),
                        pltpu.VMEM((N_HEADS, 1, tq), F32),
                        pltpu.VMEM((N_HEADS, V_ROWS, tq), F32),
                        pltpu.VMEM((N_HEADS, HEAD_DIM, tq), F32)],
        compiler_params=_params(("arbitrary", "arbitrary")),
        name="attn_prompt",
    )(qt, szt, gt, h1, ksel, vts, kwin, vtw, kc, vtc, wo, gf)


HEAD_ROWS = KV_HEADS * V7X_SUBLANES


def _attn_step_cmp_kernel(ab_ref, bias_ref, w2k_ref, w2v_ref, qbd_ref, og_ref, idx_ref,
                          *, n_cmp, n_blocks, pos):
    nck = ab_ref.shape[1]
    nbp = idx_ref.shape[2] * (-(-n_blocks // idx_ref.shape[2]))
    tok = lax.broadcasted_iota(jnp.int32, (1, nck), 1)
    mask = (CMP_STRIDE * tok + (CMP_LEN - 1) <= pos) & (tok < n_cmp)
    r = lax.broadcasted_iota(jnp.int32, (HEAD_ROWS, 1), 0)
    lane = lax.broadcasted_iota(jnp.int32, (V7X_SUBLANES, KV_WIDTH), 1)
    r8 = lax.broadcasted_iota(jnp.int32, (V7X_SUBLANES, 1), 0)
    psum = jnp.zeros((V7X_SUBLANES, nck), F32)
    for i in range(ab_ref.shape[0]):
        hk, hv = _cmp_hidden(ab_ref[i], bias_ref[0])
        k = _dot(hk, w2k_ref[...]).astype(BF16)
        v = _dot(hv, w2v_ref[...]).astype(BF16)
        s = jnp.where(mask, _dot_nt(qbd_ref[i].astype(BF16), k), NEG)
        e = jnp.exp(s - jnp.max(s, axis=1, keepdims=True)) * mask.astype(F32)
        p = e * (1.0 / jnp.maximum(jnp.sum(e, axis=1, keepdims=True), 1e-30))
        p = jnp.where(r % V7X_SUBLANES < GROUP, p, 0.0)
        o = _dot(p.astype(BF16), v)
        og = jnp.zeros((V7X_SUBLANES, KV_WIDTH), F32)
        for h in range(KV_HEADS):
            og = jnp.where(lane // HEAD_DIM == h, o[h * V7X_SUBLANES:(h + 1) * V7X_SUBLANES], og)
        og_ref[i] = og
        for h in range(KV_HEADS):
            ph = jnp.sum(p[h * V7X_SUBLANES:(h + 1) * V7X_SUBLANES], axis=0, keepdims=True)
            psum = jnp.where(r8 == KV_HEADS * i + h, ph, psum)
    ci = lax.broadcasted_iota(jnp.int32, (nck, 1), 0)
    blk = lax.broadcasted_iota(jnp.int32, (1, nbp), 1)
    member = _block_members(ci, blk, n_cmp, n_blocks).astype(BF16)
    imp = _dot(psum.astype(BF16), member)
    score = _block_scores(imp, blk, pos, n_blocks)
    slot = lax.broadcasted_iota(jnp.int32, idx_ref.shape[1:], 1)
    picked = jnp.full(idx_ref.shape[1:], -1, jnp.int32)
    for j, (top_s, first) in enumerate(_topk_pick(score, min(SEL_TOPK, n_blocks), 1)):
        picked = jnp.where(slot == j, jnp.where(top_s > -0.5 * FORCE, first, -1), picked)
    idx_ref[0] = picked


def _attn_step_cmp(ab, bias, cw, qbd, n_cmp, n_blocks, pos):
    ns, nck, _ = ab.shape
    per_step = V7X_SUBLANES // KV_HEADS
    assert ns % per_step == 0
    og, idx = pl.pallas_call(
        functools.partial(_attn_step_cmp_kernel, n_cmp=n_cmp, n_blocks=n_blocks, pos=pos),
        grid=(ns // per_step,),
        in_specs=[pl.BlockSpec((per_step, nck, AB_WIDTH), lambda b: (b, 0, 0)),
                  _full(bias), _full(cw["w2k_bd"]), _full(cw["w2v_bd"]),
                  pl.BlockSpec((per_step,) + qbd.shape[1:], lambda b: (b, 0, 0))],
        out_specs=[pl.BlockSpec((per_step, V7X_SUBLANES, KV_WIDTH), lambda b: (b, 0, 0)),
                   pl.BlockSpec((1, V7X_SUBLANES, V7X_LANES), lambda b: (b, 0, 0))],
        out_shape=[jax.ShapeDtypeStruct((ns, V7X_SUBLANES, KV_WIDTH), F32),
                   jax.ShapeDtypeStruct((ns // per_step, V7X_SUBLANES, V7X_LANES), jnp.int32)],
        compiler_params=_params(("arbitrary",)),
        name="attn_step_cmp",
    )(ab, bias, cw["w2k_bd"], cw["w2v_bd"], qbd)
    return og, idx.reshape(ns, KV_HEADS, V7X_LANES)


def _attn_step_kernel(idx_ref, pt_ref, pool_ref, q_ref, new_ref, win_ref, osel_ref, owin_ref,
                      buf, sem, *, n_past_blocks, pos, win_pos0):
    b = pl.program_id(0)
    nb = pl.num_programs(0)
    per_head = buf.shape[4] // PAGE_SIZE
    n_slots = KV_HEADS * per_head
    pages_per_seq = pt_ref.shape[0] // nb
    halves = PAGE_SIZE // SEL_BLOCK
    slot = b % 2

    def block_of(bb, h, j):
        return idx_ref[bb * n_slots + h * per_head + j]

    def copies(bb, sl):
        out = []
        for h in range(KV_HEADS):
            for j in range(per_head):
                blk = jnp.clip(block_of(bb, h, j), 0, n_past_blocks - 1)
                page = pt_ref[bb * pages_per_seq + blk // halves]
                out.append(pltpu.make_async_copy(
                    pool_ref.at[page, :, h], buf.at[sl, h, :, :, pl.ds(j * PAGE_SIZE, PAGE_SIZE)], sem.at[sl]))
        return out

    @pl.when(b == 0)
    def _():
        for cp in copies(0, 0):
            cp.start()

    @pl.when(b + 1 < nb)
    def _():
        for cp in copies(b + 1, 1 - slot):
            cp.start()

    def with_new_token(q, s, vt16, k_new, v_new, new_ok):
        s_new = jnp.where(new_ok, jnp.sum(q * k_new, axis=1, keepdims=True), NEG)
        m = jnp.maximum(jnp.max(s, axis=1, keepdims=True), s_new)
        e = jnp.exp(s - m)
        e_new = jnp.where(new_ok, jnp.exp(s_new - m), 0.0)
        den = jnp.maximum(jnp.sum(e, axis=1, keepdims=True) + e_new, 1e-30)
        return (_dot_nt(e.astype(BF16), vt16) + e_new * v_new) * (1.0 / den)

    def new_row(branch, kv, h):
        r = (branch * 2 + kv) * KV_HEADS + h
        return new_ref[0, r:r + 1, :]

    n_win = win_ref.shape[4]
    wpos = win_pos0 + lax.broadcasted_iota(jnp.int32, (1, n_win), 1)
    wmask = (wpos >= 0) & (pos - wpos >= 0) & (pos - wpos <= WINDOW)
    for h in range(KV_HEADS):
        q = q_ref[0, h]
        s = jnp.where(wmask, _dot(q.astype(BF16), win_ref[0, 0, h].astype(BF16)), NEG)
        owin_ref[0, h] = with_new_token(q, s, win_ref[0, 1, h].astype(BF16),
                                        new_row(2, 0, h), new_row(2, 1, h), True)

    for cp in copies(b, slot):
        cp.wait()

    lane = lax.broadcasted_iota(jnp.int32, (1, per_head * PAGE_SIZE), 1)
    lane_slot = lane // PAGE_SIZE
    lane_half = (lane % PAGE_SIZE) // SEL_BLOCK
    for h in range(KV_HEADS):
        code = jnp.zeros((1, per_head * PAGE_SIZE), jnp.int32)
        has_new = jnp.int32(0)
        for j in range(per_head):
            blk = block_of(b, h, j)
            past = (blk >= 0) & (blk < n_past_blocks)
            code = jnp.where(lane_slot == j, jnp.where(past, 1 + blk % halves, 0), code)
            has_new = has_new | ((blk >= n_past_blocks) & (blk * SEL_BLOCK <= pos)).astype(jnp.int32)
        q = q_ref[0, h]
        s = jnp.where(code == 1 + lane_half, _dot(q.astype(BF16), buf[slot, h, 0].astype(BF16)), NEG)
        osel_ref[0, h] = with_new_token(q, s, buf[slot, h, 1].astype(BF16),
                                        new_row(1, 0, h), new_row(1, 1, h), has_new > 0)


def _attn_step(idx, page_table, pool_t, q8, new_rows, win_t, n_past_blocks, pos, win_pos0):
    ns = q8.shape[0]
    per_head = min(SEL_TOPK, n_past_blocks + 1)
    blk = lambda a: pl.BlockSpec((1,) + a.shape[1:], lambda b, i, p: (b,) + (0,) * (a.ndim - 1))
    out = pl.BlockSpec((1, KV_HEADS, V7X_SUBLANES, HEAD_DIM), lambda b, i, p: (b, 0, 0, 0))
    out_shape = jax.ShapeDtypeStruct((ns, KV_HEADS, V7X_SUBLANES, HEAD_DIM), F32)
    return pl.pallas_call(
        functools.partial(_attn_step_kernel, n_past_blocks=n_past_blocks, pos=pos, win_pos0=win_pos0),
        grid_spec=pltpu.PrefetchScalarGridSpec(
            num_scalar_prefetch=2,
            grid=(ns,),
            in_specs=[pl.BlockSpec(memory_space=pl.ANY), blk(q8), blk(new_rows), blk(win_t)],
            out_specs=[out, out],
            scratch_shapes=[pltpu.VMEM((2, KV_HEADS, 2, HEAD_DIM, per_head * PAGE_SIZE), F32),
                            pltpu.SemaphoreType.DMA((2,))],
        ),
        out_shape=[out_shape, out_shape],
        compiler_params=_params(("arbitrary",)),
        name="attn_step",
    )(idx, page_table.reshape(-1), pool_t, q8, new_rows, win_t)


def _out_step_kernel(oc_ref, os_ref, ow_ref, g0_ref, g1_ref, g2_ref, sz_ref, h1_ref, wo_ref, gf_ref, y_ref):
    o = g0_ref[...] * oc_ref[...] + g1_ref[...] * os_ref[...] + g2_ref[...] * ow_ref[...]
    y = _dot((o * sz_ref[...]).astype(BF16), wo_ref[...])
    y_ref[...] = _rmsnorm(h1_ref[...] + y, gf_ref[...])


def _out_step(*ins):
    h1 = ins[7]
    return pl.pallas_call(
        _out_step_kernel,
        grid=(1,),
        in_specs=[_full(a) for a in ins],
        out_specs=_full(h1),
        out_shape=jax.ShapeDtypeStruct(h1.shape, F32),
        compiler_params=_params(("arbitrary",)),
        name="out_step",
    )(*ins)


def kernel(x_prompt, x_sample, cache_conv, cache_cmp_kv, cache_sel_kv, cache_win_kv, page_table,
           norm_a, conv_w_in, conv_w, conv_w_out, norm_kv, w_kv, cmp_pe, cmp_w1, cmp_w2,
           norm_b, w_qg, w_o, norm_f):
    batch, seq, d = x_prompt.shape
    dec, dec_seq, _ = x_sample.shape
    assert norm_a.shape[0] == 1 and norm_b.shape[0] == 1 and dec_seq == 1
    assert cache_conv.shape[2] == CONV_WIDTH - 1 and cache_cmp_kv.shape[1] == PAGE_SIZE
    n_pool = cache_cmp_kv.shape[0]
    n_pages = page_table.shape[1]
    past_len = n_pages * PAGE_SIZE
    dc = conv_w.shape[2]
    d_attn = N_HEADS * HEAD_DIM

    win_r, wout = conv_w_in[0].astype(BF16), conv_w_out[0].astype(BF16)
    pw = _proj_weights(w_kv, w_qg[0])
    cw = _compress_weights(cmp_pe, cmp_w1, cmp_w2)
    g_a, g_kv, g_b, g_f = norm_a[0][None], norm_kv[None], norm_b[0][None], norm_f[None]
    wo16 = w_o[0].astype(BF16)
    bias = _compress_rows(cw["pe_rows"], cw["w1"], cw["pe_rows"].shape[1])
    to_tiles = lambda a: a.transpose(0, 2, 3, 4, 1)
    from_tiles = lambda a: a.reshape(a.shape[0], 2, KV_HEADS, HEAD_DIM, a.shape[2]).transpose(0, 4, 1, 2, 3)

    n = batch * seq
    h1_p, conv_p = _conv_prompt(x_prompt.reshape(n, d), g_a, win_r, conv_w[0], wout, seq)
    conv_p = conv_p.transpose(0, 2, 1, 3).reshape(batch, CONV_WIDTH - 1, dc)
    (kvt_c, kvt_s, kvt_w, kcmp, ksel, kwin, vts, vtw, qt, szt, gt) = _proj_prompt(
        h1_p, g_kv, g_b, pw, batch, seq)
    ab_p = _compress_rows(kcmp, cw["w1"], seq)
    kc, vtc = _cmp_tokens_prompt(ab_p, bias, cw)
    y_p = _attn_prompt(qt, szt, gt, h1_p, ksel, vts, kwin, vtw, kc, vtc, wo16, g_f, batch, seq)

    prev = cache_conv[0]
    h1_s, u_s = _conv_step(x_sample.reshape(dec, d), g_a, win_r, conv_w[0], wout, prev[:, 0], prev[:, 1])
    kv_s, q_s, sz_s, gate_s = _proj_step(h1_s, g_kv, g_b, pw)
    ab_s = _compress_paged(page_table, to_tiles(cache_cmp_kv).reshape(n_pool, 2, KV_WIDTH, PAGE_SIZE),
                           cw["w1"])
    n_cmp_s = (past_len + dec_seq) // CMP_STRIDE - 1
    assert n_cmp_s + 1 == ab_s.shape[1]
    n_blocks_s = -(-(past_len + dec_seq) // SEL_BLOCK)
    q8 = jnp.pad(q_s.reshape(dec, KV_HEADS, GROUP, HEAD_DIM),
                 ((0, 0), (0, 0), (0, V7X_SUBLANES - GROUP), (0, 0)))
    qbd = jnp.einsum("bkgd,kl->bkgld", q8, jnp.eye(KV_HEADS, dtype=F32)).reshape(dec, HEAD_ROWS, KV_WIDTH)
    ogc, idx = _attn_step_cmp(ab_s, bias, cw, qbd, n_cmp_s, n_blocks_s, past_len)
    k_sel = min(SEL_TOPK, n_blocks_s)
    idx = idx[:, :KV_HEADS, :k_sel].reshape(-1)
    n_win = cache_win_kv.shape[1]
    o_sel, o_win = _attn_step(idx, page_table, to_tiles(cache_sel_kv), q8,
                              kv_s.reshape(dec, N_BRANCH * 2 * KV_HEADS, HEAD_DIM), to_tiles(cache_win_kv),
                              past_len // SEL_BLOCK, past_len, past_len - n_win)
    heads = lambda o: o[:, :, :GROUP].reshape(dec, d_attn)
    o_cmp = ogc[:, :GROUP].reshape(dec, GROUP, KV_HEADS, HEAD_DIM).transpose(0, 2, 1, 3).reshape(dec, d_attn)
    gates = gate_s[:, :N_BRANCH * N_HEADS].reshape(dec, N_HEADS, N_BRANCH)
    g_br = [jnp.repeat(gates[:, :, br], HEAD_DIM, axis=1) for br in range(N_BRANCH)]
    y_s = _out_step(o_cmp, heads(o_sel), heads(o_win), g_br[0], g_br[1], g_br[2], sz_s, h1_s, wo16, g_f)

    kv5 = lambda a, lead: a.reshape(lead + (2, KV_HEADS, HEAD_DIM))
    keep_p = min(WINDOW, seq)
    kvs3 = kv_s.reshape(dec, N_BRANCH, ROW_WIDTH)
    win_all = jnp.concatenate([cache_win_kv, kv5(kvs3[:, 2], (dec, 1))], axis=1)
    keep_s = min(WINDOW, win_all.shape[1])
    conv_s = jnp.stack([prev[:, 1], u_s], axis=1)[None]
    return (y_p.reshape(batch, seq, d), y_s.reshape(dec, 1, d), conv_p[None], conv_s,
            from_tiles(kvt_c), kv5(kvs3[:, 0], (dec, 1)),
            from_tiles(kvt_s), kv5(kvs3[:, 1], (dec, 1)),
            from_tiles(kvt_w[:, :, seq - keep_p:]), win_all[:, win_all.shape[1] - keep_s:])
```

```python
import functools
import math

import jax
import jax.numpy as jnp
from jax import lax
from jax.experimental import pallas as pl
from jax.experimental.pallas import tpu as pltpu

F32 = jnp.float32
BF16 = jnp.bfloat16

N_HEADS = 16
HEAD_DIM = 64
KV_HEADS = 4
GROUP = N_HEADS // KV_HEADS
KV_WIDTH = KV_HEADS * HEAD_DIM
ROW_WIDTH = 2 * KV_WIDTH
N_BRANCH = 3
CONV_WIDTH = 3
CMP_STRIDE = 16
CMP_LEN = 2 * CMP_STRIDE
SEL_BLOCK = 64
SEL_TOPK = 16
WINDOW = 512
PAGE_SIZE = 128
EPS = 1e-6
NEG = -1e30
FORCE = 1e9
SCALE = HEAD_DIM ** -0.5
LOG2E = math.log2(math.e)

V7X_LANES = 128
V7X_SUBLANES = 8
V7X_BF16_ROWS = 16
V7X_VMEM_LIMIT_BYTES = 56 * 1024 * 1024

CONV_CHUNK = 256
TOKEN_TILE_CONV = 512
TOKEN_TILE_PROJ = 256
Q_TILE = 256
CMP_PAGES_PER_STEP = 32
REGROUP_UNROLL = 8
QK_AHEAD = 4
GATE_ROWS = 128
V_ROWS = HEAD_DIM + V7X_BF16_ROWS

_NT = (((1,), (1,)), ((), ()))
_TN = (((0,), (0,)), ((), ()))


def _params(sem):
    return pltpu.CompilerParams(dimension_semantics=sem, vmem_limit_bytes=V7X_VMEM_LIMIT_BYTES)


def _rmsnorm(x, g):
    return x * lax.rsqrt(jnp.mean(x * x, axis=-1, keepdims=True) + EPS) * g


def _sigmoid(x):
    return 1.0 / (1.0 + jnp.exp(-x))


def _silu(x):
    return x * _sigmoid(x)


def _dot(a, b):
    return jnp.dot(a, b, preferred_element_type=F32)


def _dot_nt(a, b):
    return lax.dot_general(a, b, _NT, preferred_element_type=F32)


def _full(a):
    return pl.BlockSpec(a.shape, lambda *_: (0,) * a.ndim)


def _conv_prompt_kernel(x_ref, g_ref, win_ref, cw_ref, wout_ref, h1_ref, st_ref, ubuf_s, carry_s,
                        *, tiles_per_seq):
    i = pl.program_id(0)
    t = x_ref.shape[0]
    dc = cw_ref.shape[1]
    cw = CONV_CHUNK
    nc = dc // cw
    lo = V7X_SUBLANES
    x = x_ref[...]
    xn = _rmsnorm(x, g_ref[...]).astype(BF16)

    @pl.when(i % tiles_per_seq == 0)
    def _():
        carry_s[...] = jnp.zeros_like(carry_s)

    def in_proj(j):
        return [_dot(xn, win_ref[:, k * dc + j * cw:k * dc + (j + 1) * cw]) for k in range(4)]

    def gate(j, proj):
        bg, cg, hh, z = proj
        u = cg * hh
        ubuf_s[j, 0:lo, :] = carry_s[j]
        ubuf_s[j, lo:lo + t, :] = u
        w = cw_ref[:, j * cw:(j + 1) * cw]
        v = w[0:1] * ubuf_s[j, lo - 2:lo - 2 + t, :] + w[1:2] * ubuf_s[j, lo - 1:lo - 1 + t, :] + w[2:3] * u
        carry_s[j] = u[t - lo:t]
        st_ref[0, j] = ubuf_s[j, lo + t - 2:lo + t, :]
        return (_silu(z) * bg * v).astype(BF16)

    proj = in_proj(0)
    y = None
    for j in range(nc):
        nxt = in_proj(j + 1) if j + 1 < nc else None
        d = _dot(gate(j, proj), wout_ref[j * cw:(j + 1) * cw, :])
        y = d if y is None else y + d
        proj = nxt
    h1_ref[...] = x + y


def _conv_step_kernel(x_ref, g_ref, wb_ref, wc_ref, wh_ref, wz_ref, cw_ref, wout_ref, p0_ref, p1_ref,
                      h1_ref, u_ref, xn_s, acc_s):
    j = pl.program_id(0)

    @pl.when(j == 0)
    def _():
        xn_s[...] = _rmsnorm(x_ref[...], g_ref[...]).astype(BF16)
        acc_s[...] = jnp.zeros_like(acc_s)

    xn = xn_s[...]
    bg, z = _dot(xn, wb_ref[...]), _dot(xn, wz_ref[...])
    u = _dot(xn, wc_ref[...]) * _dot(xn, wh_ref[...])
    w = cw_ref[...]
    v = w[0:1] * p0_ref[...] + w[1:2] * p1_ref[...] + w[2:3] * u
    u_ref[...] = u
    gated = _silu(z) * bg * v
    acc_s[...] += _dot(gated.astype(BF16), wout_ref[...])

    @pl.when(j == pl.num_programs(0) - 1)
    def _():
        h1_ref[...] = x_ref[...] + acc_s[...]


def _conv_prompt(x, g, win, conv_w, wout, seq):
    n, d = x.shape
    nc = conv_w.shape[1] // CONV_CHUNK
    t = min(TOKEN_TILE_CONV, seq)
    assert seq % t == 0 and t % V7X_SUBLANES == 0
    tiles_per_seq = seq // t
    resident = lambda a: pl.BlockSpec(a.shape, lambda i: (0,) * a.ndim, pipeline_mode=pl.Buffered(1))
    return pl.pallas_call(
        functools.partial(_conv_prompt_kernel, tiles_per_seq=tiles_per_seq),
        grid=(n // t,),
        in_specs=[
            pl.BlockSpec((t, d), lambda i: (i, 0)),
            _full(g), resident(win), _full(conv_w), resident(wout),
        ],
        out_specs=[
            pl.BlockSpec((t, d), lambda i: (i, 0)),
            pl.BlockSpec((1, nc, CONV_WIDTH - 1, CONV_CHUNK), lambda i: (i // tiles_per_seq, 0, 0, 0)),
        ],
        out_shape=[
            jax.ShapeDtypeStruct((n, d), F32),
            jax.ShapeDtypeStruct((n // seq, nc, CONV_WIDTH - 1, CONV_CHUNK), F32),
        ],
        scratch_shapes=[
            pltpu.VMEM((nc, t + V7X_SUBLANES, CONV_CHUNK), F32),
            pltpu.VMEM((nc, V7X_SUBLANES, CONV_CHUNK), F32),
        ],
        compiler_params=_params(("arbitrary",)),
        name="conv_prompt",
    )(x, g, win, conv_w, wout)


def _conv_step(x, g, win, conv_w, wout, prev0, prev1):
    n, d = x.shape
    dc = conv_w.shape[1]
    nc = dc // CONV_CHUNK
    split = lambda k: pl.BlockSpec((d, CONV_CHUNK), lambda j: (0, k * nc + j))
    return pl.pallas_call(
        _conv_step_kernel,
        grid=(nc,),
        in_specs=[
            pl.BlockSpec((n, d), lambda j: (0, 0)),
            pl.BlockSpec((1, d), lambda j: (0, 0)),
            split(0), split(1), split(2), split(3),
            pl.BlockSpec((CONV_WIDTH, CONV_CHUNK), lambda j: (0, j)),
            pl.BlockSpec((CONV_CHUNK, d), lambda j: (j, 0)),
            pl.BlockSpec((n, CONV_CHUNK), lambda j: (0, j)),
            pl.BlockSpec((n, CONV_CHUNK), lambda j: (0, j)),
        ],
        out_specs=[
            pl.BlockSpec((n, d), lambda j: (0, 0)),
            pl.BlockSpec((n, CONV_CHUNK), lambda j: (0, j)),
        ],
        out_shape=[jax.ShapeDtypeStruct((n, d), F32), jax.ShapeDtypeStruct((n, dc), F32)],
        scratch_shapes=[pltpu.VMEM((n, d), BF16), pltpu.VMEM((n, d), F32)],
        compiler_params=_params(("arbitrary",)),
        name="conv_step",
    )(x, g, win, win, win, win, conv_w, wout, prev0, prev1)


def _proj_prompt_kernel(h_ref, gkv_ref, gq_ref, wrow_ref, wkvt_ref, wqzt_ref,
                        kvtc_ref, kvts_ref, kvtw_ref, kcmp_ref, ksel_ref, kwin_ref, vts_ref, vtw_ref,
                        qt_ref, szt_ref, gt_ref, *, tiles_per_seq):
    i = pl.program_id(0)
    t = h_ref.shape[0]
    d_attn = N_HEADS * HEAD_DIM
    h = h_ref[...]
    kvn = _rmsnorm(h, gkv_ref[...]).astype(BF16)
    kvt = _dot_nt(wkvt_ref[...], kvn)
    kvtc_ref[0] = kvt[0:ROW_WIDTH]
    kvts_ref[0] = kvt[ROW_WIDTH:2 * ROW_WIDTH]
    kvtw_ref[0] = kvt[2 * ROW_WIDTH:3 * ROW_WIDTH]
    ones = jnp.ones((V_ROWS - HEAD_DIM, t), BF16)
    for k in range(KV_HEADS):
        lo = ROW_WIDTH + KV_WIDTH + k * HEAD_DIM
        vts_ref[0, k, 0:HEAD_DIM] = kvt[lo:lo + HEAD_DIM].astype(BF16)
        vts_ref[0, k, HEAD_DIM:V_ROWS] = ones
        lo = 2 * ROW_WIDTH + KV_WIDTH + k * HEAD_DIM
        vtw_ref[0, k, 0:HEAD_DIM] = kvt[lo:lo + HEAD_DIM].astype(BF16)
        vtw_ref[0, k, HEAD_DIM:V_ROWS] = ones
    row = _dot(kvn, wrow_ref[...])
    for grp in range(ROW_WIDTH // V7X_LANES):
        kcmp_ref[grp] = row[:, grp * V7X_LANES:(grp + 1) * V7X_LANES]
    pos = (i % tiles_per_seq) * t + lax.broadcasted_iota(jnp.int32, (t, V7X_LANES), 0)
    lane = lax.broadcasted_iota(jnp.int32, (t, V7X_LANES), 1)
    onehot = (lane - HEAD_DIM == pos // SEL_BLOCK).astype(F32)
    for k in range(KV_HEADS):
        lo = ROW_WIDTH + k * V7X_LANES
        ksel_ref[0, k] = (row[:, lo:lo + V7X_LANES] + onehot).astype(BF16)
        lo = ROW_WIDTH + (KV_HEADS + k) * V7X_LANES
        kwin_ref[0, k] = row[:, lo:lo + V7X_LANES].astype(BF16)
    qn = _rmsnorm(h, gq_ref[...]).astype(BF16)
    qz = _dot_nt(wqzt_ref[...], qn)
    qt_ref[...] = (qz[0:d_attn] * (SCALE * LOG2E)).astype(BF16)
    szt_ref[...] = _silu(qz[d_attn:2 * d_attn]).astype(BF16)
    gt_ref[...] = _sigmoid(qz[2 * d_attn:2 * d_attn + GATE_ROWS])


def _proj_step_kernel(h_ref, gkv_ref, gq_ref, wkvt_ref, wqzt_ref, kv_ref, q_ref, sz_ref, g_ref):
    d_attn = N_HEADS * HEAD_DIM
    h = h_ref[...]
    kvn = _rmsnorm(h, gkv_ref[...]).astype(BF16)
    kv_ref[...] = _dot_nt(kvn, wkvt_ref[...])
    qn = _rmsnorm(h, gq_ref[...]).astype(BF16)
    qz = _dot_nt(qn, wqzt_ref[...])
    q_ref[...] = qz[:, 0:d_attn] * SCALE
    sz_ref[...] = _silu(qz[:, d_attn:2 * d_attn])
    g_ref[...] = _sigmoid(qz[:, 2 * d_attn:2 * d_attn + GATE_ROWS])


def _proj_weights(w_kv, w_qg):
    d = w_kv.shape[0]
    d_attn = N_HEADS * HEAD_DIM
    n_gate = N_BRANCH * N_HEADS
    wkv4 = w_kv.reshape(d, N_BRANCH, 2, KV_HEADS, HEAD_DIM)
    pad = jnp.zeros((d, KV_HEADS, V7X_LANES - HEAD_DIM), w_kv.dtype)
    ksel = jnp.concatenate([wkv4[:, 1, 0], pad], axis=2).reshape(d, KV_HEADS * V7X_LANES)
    kwin = jnp.concatenate([wkv4[:, 2, 0], pad], axis=2).reshape(d, KV_HEADS * V7X_LANES)
    wrow = jnp.concatenate([w_kv[:, 0:ROW_WIDTH], ksel, kwin], axis=1).astype(BF16)
    w_q = w_qg[:, :d_attn]
    w_g = w_qg[:, d_attn:d_attn + n_gate]
    w_z = w_qg[:, d_attn + n_gate:]
    gpad = jnp.zeros((d, GATE_ROWS - n_gate), w_qg.dtype)
    wqzt = jnp.concatenate([w_q, w_z, w_g, gpad], axis=1).T.astype(BF16)
    return dict(wrow=wrow, wkvt=w_kv.T.astype(BF16), wqzt=wqzt)


def _proj_prompt(h1, gkv, gq, pw, batch, seq):
    n, d = h1.shape
    t = TOKEN_TILE_PROJ
    assert seq % t == 0 and t == Q_TILE and seq // SEL_BLOCK <= V7X_LANES - HEAD_DIM
    d_attn = N_HEADS * HEAD_DIM
    nt = n // t
    tps = seq // t
    tok = lambda w: pl.BlockSpec((t, w), lambda i: (i, 0))
    kvt = pl.BlockSpec((1, ROW_WIDTH, t), lambda i: (i // tps, 0, i % tps))
    ktile = pl.BlockSpec((1, KV_HEADS, t, V7X_LANES), lambda i: (i, 0, 0, 0))
    vtile = pl.BlockSpec((1, KV_HEADS, V_ROWS, t), lambda i: (i, 0, 0, 0))
    col = lambda r: pl.BlockSpec((r, t), lambda i: (0, i))
    kvt_shape = jax.ShapeDtypeStruct((batch, ROW_WIDTH, seq), F32)
    return pl.pallas_call(
        functools.partial(_proj_prompt_kernel, tiles_per_seq=tps),
        grid=(nt,),
        in_specs=[tok(d), _full(gkv), _full(gq), _full(pw["wrow"]), _full(pw["wkvt"]), _full(pw["wqzt"])],
        out_specs=[kvt, kvt, kvt, pl.BlockSpec((ROW_WIDTH // V7X_LANES, t, V7X_LANES), lambda i: (0, i, 0)),
                   ktile, ktile, vtile, vtile,
                   col(d_attn), col(d_attn), col(GATE_ROWS)],
        out_shape=[
            kvt_shape, kvt_shape, kvt_shape,
            jax.ShapeDtypeStruct((ROW_WIDTH // V7X_LANES, n, V7X_LANES), F32),
            jax.ShapeDtypeStruct((nt, KV_HEADS, t, V7X_LANES), BF16),
            jax.ShapeDtypeStruct((nt, KV_HEADS, t, V7X_LANES), BF16),
            jax.ShapeDtypeStruct((nt, KV_HEADS, V_ROWS, t), BF16),
            jax.ShapeDtypeStruct((nt, KV_HEADS, V_ROWS, t), BF16),
            jax.ShapeDtypeStruct((d_attn, n), BF16),
            jax.ShapeDtypeStruct((d_attn, n), BF16),
            jax.ShapeDtypeStruct((GATE_ROWS, n), F32),
        ],
        compiler_params=_params(("arbitrary",)),
        name="proj_prompt",
    )(h1, gkv, gq, pw["wrow"], pw["wkvt"], pw["wqzt"])


def _proj_step(h1, gkv, gq, pw):
    n, d = h1.shape
    d_attn = N_HEADS * HEAD_DIM
    ins = [h1, gkv, gq, pw["wkvt"], pw["wqzt"]]
    outs = [N_BRANCH * ROW_WIDTH, d_attn, d_attn, GATE_ROWS]
    return pl.pallas_call(
        _proj_step_kernel,
        grid=(1,),
        in_specs=[_full(a) for a in ins],
        out_specs=[pl.BlockSpec((n, w), lambda i: (0, 0)) for w in outs],
        out_shape=[jax.ShapeDtypeStruct((n, w), F32) for w in outs],
        compiler_params=_params(("arbitrary",)),
        name="proj_step",
    )(*ins)


CHUNK_FEATS = CMP_STRIDE * ROW_WIDTH
AB_WIDTH = 2 * 2 * 2 * V7X_LANES


def _compress_cols(get_cols, w_ref):
    outs = []
    for c in range(2):
        for kp in range(KV_HEADS // 2):
            base = c * KV_WIDTH + kp * V7X_LANES
            lhs = jnp.concatenate([get_cols(r, base).astype(BF16) for r in range(CMP_STRIDE)], axis=1)
            outs.append(_dot(lhs, w_ref[c]))
    return jnp.concatenate(outs, axis=1)


def _compress_rows_kernel(x_ref, w_ref, ab_ref):
    m = x_ref.shape[1] // CMP_STRIDE
    ab_ref[0] = _compress_cols(
        lambda r, lo: x_ref[lo // V7X_LANES, pl.ds(r, m, stride=CMP_STRIDE), :], w_ref)


def _compress_paged_kernel(pt_ref, pool_ref, perm_ref, w_ref, ab_ref, buf, lhs_s, sem):
    step = pl.program_id(0) * pl.num_programs(1) + pl.program_id(1)
    n_steps = pl.num_programs(0) * pl.num_programs(1)
    pages = buf.shape[1]
    slot = step % 2

    def copies(s, sl):
        return [pltpu.make_async_copy(pool_ref.at[pt_ref[s * pages + p]], buf.at[sl, p], sem.at[sl])
                for p in range(pages)]

    @pl.when(step == 0)
    def _():
        for cp in copies(0, 0):
            cp.start()

    @pl.when(step + 1 < n_steps)
    def _():
        for cp in copies(step + 1, 1 - slot):
            cp.start()

    for cp in copies(step, slot):
        cp.wait()

    pair_chunks = 2 * (PAGE_SIZE // CMP_STRIDE)
    unroll = math.gcd(pages // 2, REGROUP_UNROLL)

    def regroup(it, carry):
        tiles = []
        for u in range(unroll):
            pp = it * unroll + u
            for c in range(2):
                xt = jnp.concatenate([buf[slot, 2 * pp, c], buf[slot, 2 * pp + 1, c]], axis=1).astype(BF16)
                tiles.append((pp, c, _dot_nt(perm_ref[...], xt)))
        for pp, c, t in tiles:
            t = t.astype(BF16)
            m0 = pl.multiple_of(pp * pair_chunks, pair_chunks)
            for r in range(CMP_STRIDE):
                for kp in range(KV_HEADS // 2):
                    lhs_s[c * (KV_HEADS // 2) + kp, pl.ds(m0, pair_chunks), r * V7X_LANES:(r + 1) * V7X_LANES] = (
                        t[r * pair_chunks:(r + 1) * pair_chunks, kp * V7X_LANES:(kp + 1) * V7X_LANES])
        return carry

    lax.fori_loop(0, pages // 2 // unroll, regroup, 0)
    ab_ref[0] = jnp.concatenate(
        [_dot(lhs_s[g], w_ref[g // (KV_HEADS // 2)]) for g in range(lhs_s.shape[0])], axis=1)


def _compress_weights(cmp_pe, cmp_w1, cmp_w2):
    hid = cmp_w1.shape[2]
    w1r = cmp_w1.reshape(2, 2, CMP_STRIDE, HEAD_DIM, hid)
    eye = jnp.eye(2, dtype=cmp_w1.dtype)
    w = jnp.einsum("cardE,kl->crkdalE", w1r, eye)
    w = w.reshape(2, CMP_STRIDE * 2 * HEAD_DIM, 2 * 2 * hid).astype(BF16)
    pe = cmp_pe.reshape(2, 2, CMP_STRIDE, HEAD_DIM).transpose(1, 2, 0, 3)
    pe = jnp.broadcast_to(pe[:, :, :, None, :], (2, CMP_STRIDE, 2, KV_HEADS, HEAD_DIM))
    pe_rows = jnp.zeros((V7X_SUBLANES, CHUNK_FEATS), F32).at[0:2].set(pe.reshape(2, CHUNK_FEATS))
    pe_rows = pe_rows.reshape(V7X_SUBLANES * CMP_STRIDE, ROW_WIDTH // V7X_LANES, V7X_LANES).transpose(1, 0, 2)
    eye4 = jnp.eye(KV_HEADS, dtype=cmp_w2.dtype)
    w2k = jnp.einsum("ed,kl->keld", cmp_w2[0], eye4)
    w2k_pad = jnp.concatenate([w2k, jnp.zeros_like(w2k)], axis=3)
    w2k_pad = w2k_pad.reshape(KV_HEADS * hid, KV_HEADS * V7X_LANES).astype(BF16)
    w2k_bd = w2k.reshape(KV_HEADS * hid, KV_WIDTH).astype(BF16)
    w2v_bd = jnp.einsum("ed,kl->keld", cmp_w2[1], eye4).reshape(KV_HEADS * hid, KV_WIDTH).astype(BF16)
    return dict(w1=w, pe_rows=pe_rows, w2k_pad=w2k_pad, w2k_bd=w2k_bd, w2v_bd=w2v_bd,
                w2vt=w2v_bd.T)


def _compress_rows(x, w1, seq):
    groups, n, lanes = x.shape
    ns = n // seq
    nck = seq // CMP_STRIDE
    return pl.pallas_call(
        _compress_rows_kernel,
        grid=(ns,),
        in_specs=[pl.BlockSpec((groups, seq, lanes), lambda b: (0, b, 0)), _full(w1)],
        out_specs=pl.BlockSpec((1, nck, AB_WIDTH), lambda b: (b, 0, 0)),
        out_shape=jax.ShapeDtypeStruct((ns, nck, AB_WIDTH), F32),
        compiler_params=_params(("arbitrary",)),
        name="compress_rows",
    )(x, w1)


def _compress_paged(page_table, pool, w1):
    ns, n_pages = page_table.shape
    pages = min(CMP_PAGES_PER_STEP, n_pages)
    assert n_pages % pages == 0 and pages % 2 == 0
    assert pool.shape[1:] == (2, KV_WIDTH, PAGE_SIZE) and PAGE_SIZE == V7X_LANES
    steps = n_pages // pages
    chunks_per_page = PAGE_SIZE // CMP_STRIDE
    assert 2 * chunks_per_page == V7X_BF16_ROWS
    m = pages * chunks_per_page
    t = jnp.arange(2 * PAGE_SIZE)
    r, ch = t // (2 * chunks_per_page), t % (2 * chunks_per_page)
    src = (ch // chunks_per_page) * PAGE_SIZE + (ch % chunks_per_page) * CMP_STRIDE + r
    perm = (t[None, :] == src[:, None]).astype(BF16)
    return pl.pallas_call(
        _compress_paged_kernel,
        grid_spec=pltpu.PrefetchScalarGridSpec(
            num_scalar_prefetch=1,
            grid=(ns, steps),
            in_specs=[pl.BlockSpec(memory_space=pl.ANY),
                      pl.BlockSpec(perm.shape, lambda b, s, pt: (0, 0)),
                      pl.BlockSpec(w1.shape, lambda b, s, pt: (0, 0, 0))],
            out_specs=pl.BlockSpec((1, m, AB_WIDTH), lambda b, s, pt: (b, s, 0)),
            scratch_shapes=[pltpu.VMEM((2, pages) + pool.shape[1:], F32),
                            pltpu.VMEM((ROW_WIDTH // V7X_LANES, m, CMP_STRIDE * V7X_LANES), BF16),
                            pltpu.SemaphoreType.DMA((2,))],
        ),
        out_shape=jax.ShapeDtypeStruct((ns, n_pages * chunks_per_page, AB_WIDTH), F32),
        compiler_params=_params(("arbitrary", "arbitrary")),
        name="compress_paged",
    )(page_table.reshape(-1), pool, perm, w1)


def _cmp_hidden(ab, bias):
    n = ab.shape[0]
    hid = []
    for grp in range(4):
        base = grp * 2 * V7X_LANES
        a = ab[:, base:base + V7X_LANES] + bias[0:1, base:base + V7X_LANES]
        b = ab[:, base + V7X_LANES:base + 2 * V7X_LANES] + bias[1:2, base + V7X_LANES:base + 2 * V7X_LANES]
        hid.append(_silu(a + pltpu.roll(b, n - 1, 0)))
    return (jnp.concatenate(hid[0:2], axis=1).astype(BF16),
            jnp.concatenate(hid[2:4], axis=1).astype(BF16))


def _cmp_tokens_prompt_kernel(ab_ref, bias_ref, w2k_ref, w2vt_ref, kc_ref, vtc_ref):
    hk, hv = _cmp_hidden(ab_ref[0], bias_ref[0])
    k = _dot(hk, w2k_ref[...]).astype(BF16)
    vt = _dot_nt(w2vt_ref[...], hv).astype(BF16)
    for h in range(KV_HEADS):
        kc_ref[0, h] = k[:, h * V7X_LANES:(h + 1) * V7X_LANES]
        vtc_ref[0, h] = vt[h * HEAD_DIM:(h + 1) * HEAD_DIM]


def _cmp_tokens_prompt(ab, bias, cw):
    ns, nck, _ = ab.shape
    return pl.pallas_call(
        _cmp_tokens_prompt_kernel,
        grid=(ns,),
        in_specs=[pl.BlockSpec((1, nck, AB_WIDTH), lambda b: (b, 0, 0)),
                  _full(bias), _full(cw["w2k_pad"]), _full(cw["w2vt"])],
        out_specs=[pl.BlockSpec((1, KV_HEADS, nck, V7X_LANES), lambda b: (b, 0, 0, 0)),
                   pl.BlockSpec((1, KV_HEADS, HEAD_DIM, nck), lambda b: (b, 0, 0, 0))],
        out_shape=[jax.ShapeDtypeStruct((ns, KV_HEADS, nck, V7X_LANES), BF16),
                   jax.ShapeDtypeStruct((ns, KV_HEADS, HEAD_DIM, nck), BF16)],
        compiler_params=_params(("arbitrary",)),
        name="cmp_tokens_prompt",
    )(ab, bias, cw["w2k_pad"], cw["w2vt"])


def _topk_pick(score, k, axis):
    n = score.shape[axis]
    idx = lax.broadcasted_iota(jnp.int32, score.shape, axis).astype(F32)
    s = score
    for _ in range(k):
        m = jnp.max(s, axis=axis, keepdims=True)
        first = jnp.min(jnp.where(s == m, idx, float(n)), axis=axis, keepdims=True)
        s = jnp.where(idx == first, -jnp.inf, s)
        yield m, first.astype(jnp.int32)


def _topk_mask(score, k, axis):
    n = score.shape[axis]
    idx = lax.broadcasted_iota(jnp.int32, score.shape, axis).astype(F32)
    s = score
    for _ in range(k):
        m = jnp.max(s, axis=axis, keepdims=True)
        first = jnp.min(jnp.where(s == m, idx, float(n)), axis=axis, keepdims=True)
        s = jnp.where(idx == first, -jnp.inf, s)
    return (s == -jnp.inf) & (score > -jnp.inf)


def _block_scores(imp, blk, pos, n_blocks):
    valid = blk * SEL_BLOCK <= pos
    cur = pos // SEL_BLOCK
    forced = (blk == 0) | (blk == cur) | (blk == cur - 1)
    score = jnp.where(valid & forced, FORCE, jnp.where(valid, imp, -FORCE))
    return jnp.where(blk < n_blocks, score, -jnp.inf)


def _block_members(ci, blk, n_cmp, n_blocks):
    ratio = SEL_BLOCK // CMP_STRIDE
    return ((ci >= ratio * blk - (CMP_LEN // CMP_STRIDE - 1)) & (ci <= ratio * blk + ratio - 1)
            & (ci < n_cmp) & (blk < n_blocks))


def _softmax_tiles(m_ref, acc_ref, scores, vts):
    m = m_ref[...]
    m_new = m
    for s in scores:
        m_new = jnp.maximum(m_new, jnp.max(s, axis=0, keepdims=True))
    m16 = m_new.astype(BF16)
    m_new = m16.astype(F32)
    acc = jnp.exp2(m - m_new) * acc_ref[...]
    for s, vt in zip(scores, vts):
        acc = acc + _dot(vt, jnp.exp2(s.astype(BF16) - m16))
    m_ref[...] = m_new
    acc_ref[...] = acc


def _softmax_finish(acc_ref):
    return acc_ref[0:HEAD_DIM, :] * (1.0 / jnp.maximum(acc_ref[HEAD_DIM:HEAD_DIM + 1, :], 1e-30))


def _attn_prompt_kernel(qt_ref, szt_ref, gt_ref, h1_ref, ksel_ref, vts_ref, kwin_ref, vtw_ref,
                        kc_ref, vtc_ref, wo_ref, gf_ref, y_ref, qa_s, ocmp_s, osz_s, m_s, acc_s, comb_s,
                        *, n_cmp, n_blocks):
    c = pl.program_id(1)
    tq = qt_ref.shape[1]
    ncp = kc_ref.shape[2]
    nbp = V7X_LANES - HEAD_DIM
    pos = c * tq + lax.broadcasted_iota(jnp.int32, (1, tq), 1)
    pos_rows = jnp.concatenate([pos] * GROUP, axis=1)

    kj = lax.broadcasted_iota(jnp.int32, (tq, tq), 0)
    qi = lax.broadcasted_iota(jnp.int32, (tq, tq), 1)
    causal = jnp.where(kj <= qi, 0.0, NEG)
    lower = jnp.where(kj >= qi, jnp.where(c >= 2, 0.0, NEG), NEG)
    has1 = jnp.where(c >= 1, 0.0, NEG)

    tok = lax.broadcasted_iota(jnp.int32, (ncp, 1), 0)
    cmp_mask = (CMP_STRIDE * tok + (CMP_LEN - 1) <= pos_rows) & (tok < n_cmp)
    blk = lax.broadcasted_iota(jnp.int32, (nbp, 1), 0)
    ci = lax.broadcasted_iota(jnp.int32, (1, ncp), 1)
    member_t = _block_members(ci, blk, n_cmp, n_blocks).astype(BF16)
    k_top = min(SEL_TOPK, n_blocks)

    for kvh in range(KV_HEADS):
        q_h = [qt_ref[(kvh * GROUP + g) * HEAD_DIM:(kvh * GROUP + g + 1) * HEAD_DIM, :] for g in range(GROUP)]
        q_t = jnp.concatenate(q_h, axis=1)
        qa0 = jnp.concatenate([q_t, jnp.zeros_like(q_t)], axis=0)
        sc = jnp.where(cmp_mask, _dot(kc_ref[0, kvh], qa0), NEG)
        e = jnp.exp2(sc - jnp.max(sc, axis=0, keepdims=True)) * cmp_mask.astype(F32)
        pc = e * (1.0 / jnp.maximum(jnp.sum(e, axis=0, keepdims=True), 1e-30))
        ocmp_s[kvh] = _dot(vtc_ref[0, kvh], pc.astype(BF16))
        psum = pc[:, 0:tq]
        for g in range(1, GROUP):
            psum = psum + pc[:, g * tq:(g + 1) * tq]
        imp = _dot(member_t, psum.astype(BF16))
        score = _block_scores(imp, blk, pos, n_blocks)
        picked = _topk_mask(score, k_top, 0) & (blk < n_blocks)
        mask_t = jnp.where(picked & (score > -0.5 * FORCE), 0.0, NEG).astype(BF16)
        for g in range(GROUP):
            qa_s[kvh * GROUP + g] = jnp.concatenate([q_h[g], mask_t], axis=0)

    def tiles(k_ref, v_ref, kts, biases):
        keys = [[k_ref[kt, kvh] for kt in kts] for kvh in range(KV_HEADS)]
        vts = [[v_ref[kt, kvh] for kt in kts] for kvh in range(KV_HEADS)]

        def qk(h):
            qa = qa_s[h]
            return [_dot(k, qa) if b is None else _dot(k, qa) + b for k, b in zip(keys[h // GROUP], biases)]

        scores = {h: qk(h) for h in range(QK_AHEAD)}
        for h in range(N_HEADS):
            _softmax_tiles(m_s.at[h], acc_s.at[h], scores.pop(h), vts[h // GROUP])
            if h + QK_AHEAD < N_HEADS:
                scores[h + QK_AHEAD] = qk(h + QK_AHEAD)

    def reset():
        m_s[...] = jnp.full(m_s.shape, NEG, F32)
        acc_s[...] = jnp.zeros(acc_s.shape, F32)

    gate = lambda h, br: gt_ref[h * N_BRANCH + br:h * N_BRANCH + br + 1, :]
    reset()

    def sel_pair(j, carry):
        tiles(ksel_ref, vts_ref, (2 * j, 2 * j + 1), (None, None))
        return carry

    lax.fori_loop(0, c // 2, sel_pair, 0)

    @pl.when(c % 2 == 1)
    def _():
        tiles(ksel_ref, vts_ref, (c - 1, c), (None, causal))

    @pl.when(c % 2 == 0)
    def _():
        tiles(ksel_ref, vts_ref, (c,), (causal,))
    for h in range(N_HEADS):
        kvh, g = divmod(h, GROUP)
        comb_s[h] = gate(h, 0) * ocmp_s[kvh, :, g * tq:(g + 1) * tq] + gate(h, 1) * _softmax_finish(acc_s.at[h])
    reset()
    tiles(kwin_ref, vtw_ref, (jnp.maximum(c - 2, 0), jnp.maximum(c - 1, 0), c), (lower, has1, causal))
    for h in range(N_HEADS):
        o = comb_s[h] + gate(h, 2) * _softmax_finish(acc_s.at[h])
        rows = slice(h * HEAD_DIM, (h + 1) * HEAD_DIM)
        osz_s[rows, :] = (o * szt_ref[rows, :].astype(F32)).astype(BF16)
    y = lax.dot_general(osz_s[...], wo_ref[...], _TN, preferred_element_type=F32)
    y_ref[...] = _rmsnorm(h1_ref[...] + y, gf_ref[...])


def _attn_prompt(qt, szt, gt, h1, ksel, vts, kwin, vtw, kc, vtc, wo, gf, batch, seq):
    n, d = h1.shape
    tq = Q_TILE
    nch = seq // tq
    d_attn = N_HEADS * HEAD_DIM
    n_cmp = seq // CMP_STRIDE - 1
    n_blocks = -(-seq // SEL_BLOCK)
    col = lambda r: pl.BlockSpec((r, tq), lambda b, c: (0, b * nch + c))
    seq_k = pl.BlockSpec((nch, KV_HEADS, tq, V7X_LANES), lambda b, c: (b, 0, 0, 0),
                         pipeline_mode=pl.Buffered(1))
    seq_v = pl.BlockSpec((nch, KV_HEADS, V_ROWS, tq), lambda b, c: (b, 0, 0, 0),
                         pipeline_mode=pl.Buffered(1))
    ncp = kc.shape[2]
    return pl.pallas_call(
        functools.partial(_attn_prompt_kernel, n_cmp=n_cmp, n_blocks=n_blocks),
        grid=(batch, nch),
        in_specs=[col(d_attn), col(d_attn), col(GATE_ROWS),
                  pl.BlockSpec((tq, d), lambda b, c: (b * nch + c, 0)),
                  seq_k, seq_v, seq_k, seq_v,
                  pl.BlockSpec((1, KV_HEADS, ncp, V7X_LANES), lambda b, c: (b, 0, 0, 0)),
                  pl.BlockSpec((1, KV_HEADS, HEAD_DIM, ncp), lambda b, c: (b, 0, 0, 0)),
                  _full(wo), _full(gf)],
        out_specs=pl.BlockSpec((tq, d), lambda b, c: (b * nch + c, 0)),
        out_shape=jax.ShapeDtypeStruct((n, d), F32),
        scratch_shapes=[pltpu.VMEM((N_HEADS, V7X_LANES, tq), BF16),
                        pltpu.VMEM((KV_HEADS, HEAD_DIM, GROUP * tq), F32),
                        pltpu.VMEM((d_attn, tq), BF16),
                        pltpu.VMEM((N_HEADS, 1, tq), F32),
                        pltpu.VMEM((N_HEADS, V_ROWS, tq), F32),
                        pltpu.VMEM((N_HEADS, HEAD_DIM, tq), F32)],
        compiler_params=_params(("arbitrary", "arbitrary")),
        name="attn_prompt",
    )(qt, szt, gt, h1, ksel, vts, kwin, vtw, kc, vtc, wo, gf)


HEAD_ROWS = KV_HEADS * V7X_SUBLANES


def _attn_step_cmp_kernel(ab_ref, bias_ref, w2k_ref, w2v_ref, qbd_ref, og_ref, idx_ref,
                          *, n_cmp, n_blocks, pos):
    nck = ab_ref.shape[1]
    nbp = idx_ref.shape[2] * (-(-n_blocks // idx_ref.shape[2]))
    tok = lax.broadcasted_iota(jnp.int32, (1, nck), 1)
    mask = (CMP_STRIDE * tok + (CMP_LEN - 1) <= pos) & (tok < n_cmp)
    r = lax.broadcasted_iota(jnp.int32, (HEAD_ROWS, 1), 0)
    lane = lax.broadcasted_iota(jnp.int32, (V7X_SUBLANES, KV_WIDTH), 1)
    r8 = lax.broadcasted_iota(jnp.int32, (V7X_SUBLANES, 1), 0)
    psum = jnp.zeros((V7X_SUBLANES, nck), F32)
    for i in range(ab_ref.shape[0]):
        hk, hv = _cmp_hidden(ab_ref[i], bias_ref[0])
        k = _dot(hk, w2k_ref[...]).astype(BF16)
        v = _dot(hv, w2v_ref[...]).astype(BF16)
        s = jnp.where(mask, _dot_nt(qbd_ref[i].astype(BF16), k), NEG)
        e = jnp.exp(s - jnp.max(s, axis=1, keepdims=True)) * mask.astype(F32)
        p = e * (1.0 / jnp.maximum(jnp.sum(e, axis=1, keepdims=True), 1e-30))
        p = jnp.where(r % V7X_SUBLANES < GROUP, p, 0.0)
        o = _dot(p.astype(BF16), v)
        og = jnp.zeros((V7X_SUBLANES, KV_WIDTH), F32)
        for h in range(KV_HEADS):
            og = jnp.where(lane // HEAD_DIM == h, o[h * V7X_SUBLANES:(h + 1) * V7X_SUBLANES], og)
        og_ref[i] = og
        for h in range(KV_HEADS):
            ph = jnp.sum(p[h * V7X_SUBLANES:(h + 1) * V7X_SUBLANES], axis=0, keepdims=True)
            psum = jnp.where(r8 == KV_HEADS * i + h, ph, psum)
    ci = lax.broadcasted_iota(jnp.int32, (nck, 1), 0)
    blk = lax.broadcasted_iota(jnp.int32, (1, nbp), 1)
    member = _block_members(ci, blk, n_cmp, n_blocks).astype(BF16)
    imp = _dot(psum.astype(BF16), member)
    score = _block_scores(imp, blk, pos, n_blocks)
    slot = lax.broadcasted_iota(jnp.int32, idx_ref.shape[1:], 1)
    picked = jnp.full(idx_ref.shape[1:], -1, jnp.int32)
    for j, (top_s, first) in enumerate(_topk_pick(score, min(SEL_TOPK, n_blocks), 1)):
        picked = jnp.where(slot == j, jnp.where(top_s > -0.5 * FORCE, first, -1), picked)
    idx_ref[0] = picked


def _attn_step_cmp(ab, bias, cw, qbd, n_cmp, n_blocks, pos):
    ns, nck, _ = ab.shape
    per_step = V7X_SUBLANES // KV_HEADS
    assert ns % per_step == 0
    og, idx = pl.pallas_call(
        functools.partial(_attn_step_cmp_kernel, n_cmp=n_cmp, n_blocks=n_blocks, pos=pos),
        grid=(ns // per_step,),
        in_specs=[pl.BlockSpec((per_step, nck, AB_WIDTH), lambda b: (b, 0, 0)),
                  _full(bias), _full(cw["w2k_bd"]), _full(cw["w2v_bd"]),
                  pl.BlockSpec((per_step,) + qbd.shape[1:], lambda b: (b, 0, 0))],
        out_specs=[pl.BlockSpec((per_step, V7X_SUBLANES, KV_WIDTH), lambda b: (b, 0, 0)),
                   pl.BlockSpec((1, V7X_SUBLANES, V7X_LANES), lambda b: (b, 0, 0))],
        out_shape=[jax.ShapeDtypeStruct((ns, V7X_SUBLANES, KV_WIDTH), F32),
                   jax.ShapeDtypeStruct((ns // per_step, V7X_SUBLANES, V7X_LANES), jnp.int32)],
        compiler_params=_params(("arbitrary",)),
        name="attn_step_cmp",
    )(ab, bias, cw["w2k_bd"], cw["w2v_bd"], qbd)
    return og, idx.reshape(ns, KV_HEADS, V7X_LANES)


def _attn_step_kernel(idx_ref, pt_ref, pool_ref, q_ref, new_ref, win_ref, osel_ref, owin_ref,
                      buf, sem, *, n_past_blocks, pos, win_pos0):
    b = pl.program_id(0)
    nb = pl.num_programs(0)
    per_head = buf.shape[4] // PAGE_SIZE
    n_slots = KV_HEADS * per_head
    pages_per_seq = pt_ref.shape[0] // nb
    halves = PAGE_SIZE // SEL_BLOCK
    slot = b % 2

    def block_of(bb, h, j):
        return idx_ref[bb * n_slots + h * per_head + j]

    def copies(bb, sl):
        out = []
        for h in range(KV_HEADS):
            for j in range(per_head):
                blk = jnp.clip(block_of(bb, h, j), 0, n_past_blocks - 1)
                page = pt_ref[bb * pages_per_seq + blk // halves]
                out.append(pltpu.make_async_copy(
                    pool_ref.at[page, :, h], buf.at[sl, h, :, :, pl.ds(j * PAGE_SIZE, PAGE_SIZE)], sem.at[sl]))
        return out

    @pl.when(b == 0)
    def _():
        for cp in copies(0, 0):
            cp.start()

    @pl.when(b + 1 < nb)
    def _():
        for cp in copies(b + 1, 1 - slot):
            cp.start()

    def with_new_token(q, s, vt16, k_new, v_new, new_ok):
        s_new = jnp.where(new_ok, jnp.sum(q * k_new, axis=1, keepdims=True), NEG)
        m = jnp.maximum(jnp.max(s, axis=1, keepdims=True), s_new)
        e = jnp.exp(s - m)
        e_new = jnp.where(new_ok, jnp.exp(s_new - m), 0.0)
        den = jnp.maximum(jnp.sum(e, axis=1, keepdims=True) + e_new, 1e-30)
        return (_dot_nt(e.astype(BF16), vt16) + e_new * v_new) * (1.0 / den)

    def new_row(branch, kv, h):
        r = (branch * 2 + kv) * KV_HEADS + h
        return new_ref[0, r:r + 1, :]

    n_win = win_ref.shape[4]
    wpos = win_pos0 + lax.broadcasted_iota(jnp.int32, (1, n_win), 1)
    wmask = (wpos >= 0) & (pos - wpos >= 0) & (pos - wpos <= WINDOW)
    for h in range(KV_HEADS):
        q = q_ref[0, h]
        s = jnp.where(wmask, _dot(q.astype(BF16), win_ref[0, 0, h].astype(BF16)), NEG)
        owin_ref[0, h] = with_new_token(q, s, win_ref[0, 1, h].astype(BF16),
                                        new_row(2, 0, h), new_row(2, 1, h), True)

    for cp in copies(b, slot):
        cp.wait()

    lane = lax.broadcasted_iota(jnp.int32, (1, per_head * PAGE_SIZE), 1)
    lane_slot = lane // PAGE_SIZE
    lane_half = (lane % PAGE_SIZE) // SEL_BLOCK
    for h in range(KV_HEADS):
        code = jnp.zeros((1, per_head * PAGE_SIZE), jnp.int32)
        has_new = jnp.int32(0)
        for j in range(per_head):
            blk = block_of(b, h, j)
            past = (blk >= 0) & (blk < n_past_blocks)
            code = jnp.where(lane_slot == j, jnp.where(past, 1 + blk % halves, 0), code)
            has_new = has_new | ((blk >= n_past_blocks) & (blk * SEL_BLOCK <= pos)).astype(jnp.int32)
        q = q_ref[0, h]
        s = jnp.where(code == 1 + lane_half, _dot(q.astype(BF16), buf[slot, h, 0].astype(BF16)), NEG)
        osel_ref[0, h] = with_new_token(q, s, buf[slot, h, 1].astype(BF16),
                                        new_row(1, 0, h), new_row(1, 1, h), has_new > 0)


def _attn_step(idx, page_table, pool_t, q8, new_rows, win_t, n_past_blocks, pos, win_pos0):
    ns = q8.shape[0]
    per_head = min(SEL_TOPK, n_past_blocks + 1)
    blk = lambda a: pl.BlockSpec((1,) + a.shape[1:], lambda b, i, p: (b,) + (0,) * (a.ndim - 1))
    out = pl.BlockSpec((1, KV_HEADS, V7X_SUBLANES, HEAD_DIM), lambda b, i, p: (b, 0, 0, 0))
    out_shape = jax.ShapeDtypeStruct((ns, KV_HEADS, V7X_SUBLANES, HEAD_DIM), F32)
    return pl.pallas_call(
        functools.partial(_attn_step_kernel, n_past_blocks=n_past_blocks, pos=pos, win_pos0=win_pos0),
        grid_spec=pltpu.PrefetchScalarGridSpec(
            num_scalar_prefetch=2,
            grid=(ns,),
            in_specs=[pl.BlockSpec(memory_space=pl.ANY), blk(q8), blk(new_rows), blk(win_t)],
            out_specs=[out, out],
            scratch_shapes=[pltpu.VMEM((2, KV_HEADS, 2, HEAD_DIM, per_head * PAGE_SIZE), F32),
                            pltpu.SemaphoreType.DMA((2,))],
        ),
        out_shape=[out_shape, out_shape],
        compiler_params=_params(("arbitrary",)),
        name="attn_step",
    )(idx, page_table.reshape(-1), pool_t, q8, new_rows, win_t)


def _out_step_kernel(oc_ref, os_ref, ow_ref, g0_ref, g1_ref, g2_ref, sz_ref, h1_ref, wo_ref, gf_ref, y_ref):
    o = g0_ref[...] * oc_ref[...] + g1_ref[...] * os_ref[...] + g2_ref[...] * ow_ref[...]
    y = _dot((o * sz_ref[...]).astype(BF16), wo_ref[...])
    y_ref[...] = _rmsnorm(h1_ref[...] + y, gf_ref[...])


def _out_step(*ins):
    h1 = ins[7]
    return pl.pallas_call(
        _out_step_kernel,
        grid=(1,),
        in_specs=[_full(a) for a in ins],
        out_specs=_full(h1),
        out_shape=jax.ShapeDtypeStruct(h1.shape, F32),
        compiler_params=_params(("arbitrary",)),
        name="out_step",
    )(*ins)


def kernel(x_prompt, x_sample, cache_conv, cache_cmp_kv, cache_sel_kv, cache_win_kv, page_table,
           norm_a, conv_w_in, conv_w, conv_w_out, norm_kv, w_kv, cmp_pe, cmp_w1, cmp_w2,
           norm_b, w_qg, w_o, norm_f):
    batch, seq, d = x_prompt.shape
    dec, dec_seq, _ = x_sample.shape
    assert norm_a.shape[0] == 1 and norm_b.shape[0] == 1 and dec_seq == 1
    assert cache_conv.shape[2] == CONV_WIDTH - 1 and cache_cmp_kv.shape[1] == PAGE_SIZE
    n_pool = cache_cmp_kv.shape[0]
    n_pages = page_table.shape[1]
    past_len = n_pages * PAGE_SIZE
    dc = conv_w.shape[2]
    d_attn = N_HEADS * HEAD_DIM

    win_r, wout = conv_w_in[0].astype(BF16), conv_w_out[0].astype(BF16)
    pw = _proj_weights(w_kv, w_qg[0])
    cw = _compress_weights(cmp_pe, cmp_w1, cmp_w2)
    g_a, g_kv, g_b, g_f = norm_a[0][None], norm_kv[None], norm_b[0][None], norm_f[None]
    wo16 = w_o[0].astype(BF16)
    bias = _compress_rows(cw["pe_rows"], cw["w1"], cw["pe_rows"].shape[1])
    to_tiles = lambda a: a.transpose(0, 2, 3, 4, 1)
    from_tiles = lambda a: a.reshape(a.shape[0], 2, KV_HEADS, HEAD_DIM, a.shape[2]).transpose(0, 4, 1, 2, 3)

    n = batch * seq
    h1_p, conv_p = _conv_prompt(x_prompt.reshape(n, d), g_a, win_r, conv_w[0], wout, seq)
    conv_p = conv_p.transpose(0, 2, 1, 3).reshape(batch, CONV_WIDTH - 1, dc)
    (kvt_c, kvt_s, kvt_w, kcmp, ksel, kwin, vts, vtw, qt, szt, gt) = _proj_prompt(
        h1_p, g_kv, g_b, pw, batch, seq)
    ab_p = _compress_rows(kcmp, cw["w1"], seq)
    kc, vtc = _cmp_tokens_prompt(ab_p, bias, cw)
    y_p = _attn_prompt(qt, szt, gt, h1_p, ksel, vts, kwin, vtw, kc, vtc, wo16, g_f, batch, seq)

    prev = cache_conv[0]
    h1_s, u_s = _conv_step(x_sample.reshape(dec, d), g_a, win_r, conv_w[0], wout, prev[:, 0], prev[:, 1])
    kv_s, q_s, sz_s, gate_s = _proj_step(h1_s, g_kv, g_b, pw)
    ab_s = _compress_paged(page_table, to_tiles(cache_cmp_kv).reshape(n_pool, 2, KV_WIDTH, PAGE_SIZE),
                           cw["w1"])
    n_cmp_s = (past_len + dec_seq) // CMP_STRIDE - 1
    assert n_cmp_s + 1 == ab_s.shape[1]
    n_blocks_s = -(-(past_len + dec_seq) // SEL_BLOCK)
    q8 = jnp.pad(q_s.reshape(dec, KV_HEADS, GROUP, HEAD_DIM),
                 ((0, 0), (0, 0), (0, V7X_SUBLANES - GROUP), (0, 0)))
    qbd = jnp.einsum("bkgd,kl->bkgld", q8, jnp.eye(KV_HEADS, dtype=F32)).reshape(dec, HEAD_ROWS, KV_WIDTH)
    ogc, idx = _attn_step_cmp(ab_s, bias, cw, qbd, n_cmp_s, n_blocks_s, past_len)
    k_sel = min(SEL_TOPK, n_blocks_s)
    idx = idx[:, :KV_HEADS, :k_sel].reshape(-1)
    n_win = cache_win_kv.shape[1]
    o_sel, o_win = _attn_step(idx, page_table, to_tiles(cache_sel_kv), q8,
                              kv_s.reshape(dec, N_BRANCH * 2 * KV_HEADS, HEAD_DIM), to_tiles(cache_win_kv),
                              past_len // SEL_BLOCK, past_len, past_len - n_win)
    heads = lambda o: o[:, :, :GROUP].reshape(dec, d_attn)
    o_cmp = ogc[:, :GROUP].reshape(dec, GROUP, KV_HEADS, HEAD_DIM).transpose(0, 2, 1, 3).reshape(dec, d_attn)
    gates = gate_s[:, :N_BRANCH * N_HEADS].reshape(dec, N_HEADS, N_BRANCH)
    g_br = [jnp.repeat(gates[:, :, br], HEAD_DIM, axis=1) for br in range(N_BRANCH)]
    y_s = _out_step(o_cmp, heads(o_sel), heads(o_win), g_br[0], g_br[1], g_br[2], sz_s, h1_s, wo16, g_f)

    kv5 = lambda a, lead: a.reshape(lead + (2, KV_HEADS, HEAD_DIM))
    keep_p = min(WINDOW, seq)
    kvs3 = kv_s.reshape(dec, N_BRANCH, ROW_WIDTH)
    win_all = jnp.concatenate([cache_win_kv, kv5(kvs3[:, 2], (dec, 1))], axis=1)
    keep_s = min(WINDOW, win_all.shape[1])
    conv_s = jnp.stack([prev[:, 1], u_s], axis=1)[None]
    return (y_p.reshape(batch, seq, d), y_s.reshape(dec, 1, d), conv_p[None], conv_s,
            from_tiles(kvt_c), kv5(kvs3[:, 0], (dec, 1)),
            from_tiles(kvt_s), kv5(kvs3[:, 1], (dec, 1)),
            from_tiles(kvt_w[:, :, seq - keep_p:]), win_all[:, win_all.shape[1] - keep_s:])
```

```python
import functools
import math

import jax
import jax.numpy as jnp
from jax import lax
from jax.experimental import pallas as pl
from jax.experimental.pallas import tpu as pltpu

F32 = jnp.float32
BF16 = jnp.bfloat16

N_HEADS = 16
HEAD_DIM = 64
KV_HEADS = 4
GROUP = N_HEADS // KV_HEADS
KV_WIDTH = KV_HEADS * HEAD_DIM
ROW_WIDTH = 2 * KV_WIDTH
N_BRANCH = 3
CONV_WIDTH = 3
CMP_STRIDE = 16
CMP_LEN = 2 * CMP_STRIDE
SEL_BLOCK = 64
SEL_TOPK = 16
WINDOW = 512
PAGE_SIZE = 128
EPS = 1e-6
NEG = -1e30
FORCE = 1e9
SCALE = HEAD_DIM ** -0.5
LOG2E = math.log2(math.e)

V7X_LANES = 128
V7X_SUBLANES = 8
V7X_BF16_ROWS = 16
V7X_VMEM_LIMIT_BYTES = 56 * 1024 * 1024

CONV_CHUNK = 256
TOKEN_TILE_CONV = 512
TOKEN_TILE_PROJ = 256
Q_TILE = 256
CMP_PAGES_PER_STEP = 32
REGROUP_UNROLL = 8
QK_AHEAD = 4
GATE_ROWS = 128
V_ROWS = HEAD_DIM + V7X_BF16_ROWS

_NT = (((1,), (1,)), ((), ()))
_TN = (((0,), (0,)), ((), ()))


def _params(sem):
    return pltpu.CompilerParams(dimension_semantics=sem, vmem_limit_bytes=V7X_VMEM_LIMIT_BYTES)


def _rmsnorm(x, g):
    return x * lax.rsqrt(jnp.mean(x * x, axis=-1, keepdims=True) + EPS) * g


def _sigmoid(x):
    return 1.0 / (1.0 + jnp.exp(-x))


def _silu(x):
    return x * _sigmoid(x)


def _dot(a, b):
    return jnp.dot(a, b, preferred_element_type=F32)


def _dot_nt(a, b):
    return lax.dot_general(a, b, _NT, preferred_element_type=F32)


def _full(a):
    return pl.BlockSpec(a.shape, lambda *_: (0,) * a.ndim)


def _conv_prompt_kernel(x_ref, g_ref, win_ref, cw_ref, wout_ref, h1_ref, st_ref, ubuf_s, carry_s,
                        *, tiles_per_seq):
    i = pl.program_id(0)
    t = x_ref.shape[0]
    dc = cw_ref.shape[1]
    cw = CONV_CHUNK
    nc = dc // cw
    lo = V7X_SUBLANES
    x = x_ref[...]
    xn = _rmsnorm(x, g_ref[...]).astype(BF16)

    @pl.when(i % tiles_per_seq == 0)
    def _():
        carry_s[...] = jnp.zeros_like(carry_s)

    def in_proj(j):
        return [_dot(xn, win_ref[:, k * dc + j * cw:k * dc + (j + 1) * cw]) for k in range(4)]

    def gate(j, proj):
        bg, cg, hh, z = proj
        u = cg * hh
        ubuf_s[j, 0:lo, :] = carry_s[j]
        ubuf_s[j, lo:lo + t, :] = u
        w = cw_ref[:, j * cw:(j + 1) * cw]
        v = w[0:1] * ubuf_s[j, lo - 2:lo - 2 + t, :] + w[1:2] * ubuf_s[j, lo - 1:lo - 1 + t, :] + w[2:3] * u
        carry_s[j] = u[t - lo:t]
        st_ref[0, j] = ubuf_s[j, lo + t - 2:lo + t, :]
        return (_silu(z) * bg * v).astype(BF16)

    proj = in_proj(0)
    y = None
    for j in range(nc):
        nxt = in_proj(j + 1) if j + 1 < nc else None
        d = _dot(gate(j, proj), wout_ref[j * cw:(j + 1) * cw, :])
        y = d if y is None else y + d
        proj = nxt
    h1_ref[...] = x + y


def _conv_step_kernel(x_ref, g_ref, wb_ref, wc_ref, wh_ref, wz_ref, cw_ref, wout_ref, p0_ref, p1_ref,
                      h1_ref, u_ref, xn_s, acc_s):
    j = pl.program_id(0)

    @pl.when(j == 0)
    def _():
        xn_s[...] = _rmsnorm(x_ref[...], g_ref[...]).astype(BF16)
        acc_s[...] = jnp.zeros_like(acc_s)

    xn = xn_s[...]
    bg, z = _dot(xn, wb_ref[...]), _dot(xn, wz_ref[...])
    u = _dot(xn, wc_ref[...]) * _dot(xn, wh_ref[...])
    w = cw_ref[...]
    v = w[0:1] * p0_ref[...] + w[1:2] * p1_ref[...] + w[2:3] * u
    u_ref[...] = u
    gated = _silu(z) * bg * v
    acc_s[...] += _dot(gated.astype(BF16), wout_ref[...])

    @pl.when(j == pl.num_programs(0) - 1)
    def _():
        h1_ref[...] = x_ref[...] + acc_s[...]


def _conv_prompt(x, g, win, conv_w, wout, seq):
    n, d = x.shape
    nc = conv_w.shape[1] // CONV_CHUNK
    t = min(TOKEN_TILE_CONV, seq)
    assert seq % t == 0 and t % V7X_SUBLANES == 0
    tiles_per_seq = seq // t
    resident = lambda a: pl.BlockSpec(a.shape, lambda i: (0,) * a.ndim, pipeline_mode=pl.Buffered(1))
    return pl.pallas_call(
        functools.partial(_conv_prompt_kernel, tiles_per_seq=tiles_per_seq),
        grid=(n // t,),
        in_specs=[
            pl.BlockSpec((t, d), lambda i: (i, 0)),
            _full(g), resident(win), _full(conv_w), resident(wout),
        ],
        out_specs=[
            pl.BlockSpec((t, d), lambda i: (i, 0)),
            pl.BlockSpec((1, nc, CONV_WIDTH - 1, CONV_CHUNK), lambda i: (i // tiles_per_seq, 0, 0, 0)),
        ],
        out_shape=[
            jax.ShapeDtypeStruct((n, d), F32),
            jax.ShapeDtypeStruct((n // seq, nc, CONV_WIDTH - 1, CONV_CHUNK), F32),
        ],
        scratch_shapes=[
            pltpu.VMEM((nc, t + V7X_SUBLANES, CONV_CHUNK), F32),
            pltpu.VMEM((nc, V7X_SUBLANES, CONV_CHUNK), F32),
        ],
        compiler_params=_params(("arbitrary",)),
        name="conv_prompt",
    )(x, g, win, conv_w, wout)


def _conv_step(x, g, win, conv_w, wout, prev0, prev1):
    n, d = x.shape
    dc = conv_w.shape[1]
    nc = dc // CONV_CHUNK
    split = lambda k: pl.BlockSpec((d, CONV_CHUNK), lambda j: (0, k * nc + j))
    return pl.pallas_call(
        _conv_step_kernel,
        grid=(nc,),
        in_specs=[
            pl.BlockSpec((n, d), lambda j: (0, 0)),
            pl.BlockSpec((1, d), lambda j: (0, 0)),
            split(0), split(1), split(2), split(3),
            pl.BlockSpec((CONV_WIDTH, CONV_CHUNK), lambda j: (0, j)),
            pl.BlockSpec((CONV_CHUNK, d), lambda j: (j, 0)),
            pl.BlockSpec((n, CONV_CHUNK), lambda j: (0, j)),
            pl.BlockSpec((n, CONV_CHUNK), lambda j: (0, j)),
        ],
        out_specs=[
            pl.BlockSpec((n, d), lambda j: (0, 0)),
            pl.BlockSpec((n, CONV_CHUNK), lambda j: (0, j)),
        ],
        out_shape=[jax.ShapeDtypeStruct((n, d), F32), jax.ShapeDtypeStruct((n, dc), F32)],
        scratch_shapes=[pltpu.VMEM((n, d), BF16), pltpu.VMEM((n, d), F32)],
        compiler_params=_params(("arbitrary",)),
        name="conv_step",
    )(x, g, win, win, win, win, conv_w, wout, prev0, prev1)


def _proj_prompt_kernel(h_ref, gkv_ref, gq_ref, wrow_ref, wkvt_ref, wqzt_ref,
                        kvtc_ref, kvts_ref, kvtw_ref, kcmp_ref, ksel_ref, kwin_ref, vts_ref, vtw_ref,
                        qt_ref, szt_ref, gt_ref, *, tiles_per_seq):
    i = pl.program_id(0)
    t = h_ref.shape[0]
    d_attn = N_HEADS * HEAD_DIM
    h = h_ref[...]
    kvn = _rmsnorm(h, gkv_ref[...]).astype(BF16)
    kvt = _dot_nt(wkvt_ref[...], kvn)
    kvtc_ref[0] = kvt[0:ROW_WIDTH]
    kvts_ref[0] = kvt[ROW_WIDTH:2 * ROW_WIDTH]
    kvtw_ref[0] = kvt[2 * ROW_WIDTH:3 * ROW_WIDTH]
    ones = jnp.ones((V_ROWS - HEAD_DIM, t), BF16)
    for k in range(KV_HEADS):
        lo = ROW_WIDTH + KV_WIDTH + k * HEAD_DIM
        vts_ref[0, k, 0:HEAD_DIM] = kvt[lo:lo + HEAD_DIM].astype(BF16)
        vts_ref[0, k, HEAD_DIM:V_ROWS] = ones
        lo = 2 * ROW_WIDTH + KV_WIDTH + k * HEAD_DIM
        vtw_ref[0, k, 0:HEAD_DIM] = kvt[lo:lo + HEAD_DIM].astype(BF16)
        vtw_ref[0, k, HEAD_DIM:V_ROWS] = ones
    row = _dot(kvn, wrow_ref[...])
    for grp in range(ROW_WIDTH // V7X_LANES):
        kcmp_ref[grp] = row[:, grp * V7X_LANES:(grp + 1) * V7X_LANES]
    pos = (i % tiles_per_seq) * t + lax.broadcasted_iota(jnp.int32, (t, V7X_LANES), 0)
    lane = lax.broadcasted_iota(jnp.int32, (t, V7X_LANES), 1)
    onehot = (lane - HEAD_DIM == pos // SEL_BLOCK).astype(F32)
    for k in range(KV_HEADS):
        lo = ROW_WIDTH + k * V7X_LANES
        ksel_ref[0, k] = (row[:, lo:lo + V7X_LANES] + onehot).astype(BF16)
        lo = ROW_WIDTH + (KV_HEADS + k) * V7X_LANES
        kwin_ref[0, k] = row[:, lo:lo + V7X_LANES].astype(BF16)
    qn = _rmsnorm(h, gq_ref[...]).astype(BF16)
    qz = _dot_nt(wqzt_ref[...], qn)
    qt_ref[...] = (qz[0:d_attn] * (SCALE * LOG2E)).astype(BF16)
    szt_ref[...] = _silu(qz[d_attn:2 * d_attn]).astype(BF16)
    gt_ref[...] = _sigmoid(qz[2 * d_attn:2 * d_attn + GATE_ROWS])


def _proj_step_kernel(h_ref, gkv_ref, gq_ref, wkvt_ref, wqzt_ref, kv_ref, q_ref, sz_ref, g_ref):
    d_attn = N_HEADS * HEAD_DIM
    h = h_ref[...]
    kvn = _rmsnorm(h, gkv_ref[...]).astype(BF16)
    kv_ref[...] = _dot_nt(kvn, wkvt_ref[...])
    qn = _rmsnorm(h, gq_ref[...]).astype(BF16)
    qz = _dot_nt(qn, wqzt_ref[...])
    q_ref[...] = qz[:, 0:d_attn] * SCALE
    sz_ref[...] = _silu(qz[:, d_attn:2 * d_attn])
    g_ref[...] = _sigmoid(qz[:, 2 * d_attn:2 * d_attn + GATE_ROWS])


def _proj_weights(w_kv, w_qg):
    d = w_kv.shape[0]
    d_attn = N_HEADS * HEAD_DIM
    n_gate = N_BRANCH * N_HEADS
    wkv4 = w_kv.reshape(d, N_BRANCH, 2, KV_HEADS, HEAD_DIM)
    pad = jnp.zeros((d, KV_HEADS, V7X_LANES - HEAD_DIM), w_kv.dtype)
    ksel = jnp.concatenate([wkv4[:, 1, 0], pad], axis=2).reshape(d, KV_HEADS * V7X_LANES)
    kwin = jnp.concatenate([wkv4[:, 2, 0], pad], axis=2).reshape(d, KV_HEADS * V7X_LANES)
    wrow = jnp.concatenate([w_kv[:, 0:ROW_WIDTH], ksel, kwin], axis=1).astype(BF16)
    w_q = w_qg[:, :d_attn]
    w_g = w_qg[:, d_attn:d_attn + n_gate]
    w_z = w_qg[:, d_attn + n_gate:]
    gpad = jnp.zeros((d, GATE_ROWS - n_gate), w_qg.dtype)
    wqzt = jnp.concatenate([w_q, w_z, w_g, gpad], axis=1).T.astype(BF16)
    return dict(wrow=wrow, wkvt=w_kv.T.astype(BF16), wqzt=wqzt)


def _proj_prompt(h1, gkv, gq, pw, batch, seq):
    n, d = h1.shape
    t = TOKEN_TILE_PROJ
    assert seq % t == 0 and t == Q_TILE and seq // SEL_BLOCK <= V7X_LANES - HEAD_DIM
    d_attn = N_HEADS * HEAD_DIM
    nt = n // t
    tps = seq // t
    tok = lambda w: pl.BlockSpec((t, w), lambda i: (i, 0))
    kvt = pl.BlockSpec((1, ROW_WIDTH, t), lambda i: (i // tps, 0, i % tps))
    ktile = pl.BlockSpec((1, KV_HEADS, t, V7X_LANES), lambda i: (i, 0, 0, 0))
    vtile = pl.BlockSpec((1, KV_HEADS, V_ROWS, t), lambda i: (i, 0, 0, 0))
    col = lambda r: pl.BlockSpec((r, t), lambda i: (0, i))
    kvt_shape = jax.ShapeDtypeStruct((batch, ROW_WIDTH, seq), F32)
    return pl.pallas_call(
        functools.partial(_proj_prompt_kernel, tiles_per_seq=tps),
        grid=(nt,),
        in_specs=[tok(d), _full(gkv), _full(gq), _full(pw["wrow"]), _full(pw["wkvt"]), _full(pw["wqzt"])],
        out_specs=[kvt, kvt, kvt, pl.BlockSpec((ROW_WIDTH // V7X_LANES, t, V7X_LANES), lambda i: (0, i, 0)),
                   ktile, ktile, vtile, vtile,
                   col(d_attn), col(d_attn), col(GATE_ROWS)],
        out_shape=[
            kvt_shape, kvt_shape, kvt_shape,
            jax.ShapeDtypeStruct((ROW_WIDTH // V7X_LANES, n, V7X_LANES), F32),
            jax.ShapeDtypeStruct((nt, KV_HEADS, t, V7X_LANES), BF16),
            jax.ShapeDtypeStruct((nt, KV_HEADS, t, V7X_LANES), BF16),
            jax.ShapeDtypeStruct((nt, KV_HEADS, V_ROWS, t), BF16),
            jax.ShapeDtypeStruct((nt, KV_HEADS, V_ROWS, t), BF16),
            jax.ShapeDtypeStruct((d_attn, n), BF16),
            jax.ShapeDtypeStruct((d_attn, n), BF16),
            jax.ShapeDtypeStruct((GATE_ROWS, n), F32),
        ],
        compiler_params=_params(("arbitrary",)),
        name="proj_prompt",
    )(h1, gkv, gq, pw["wrow"], pw["wkvt"], pw["wqzt"])


def _proj_step(h1, gkv, gq, pw):
    n, d = h1.shape
    d_attn = N_HEADS * HEAD_DIM
    ins = [h1, gkv, gq, pw["wkvt"], pw["wqzt"]]
    outs = [N_BRANCH * ROW_WIDTH, d_attn, d_attn, GATE_ROWS]
    return pl.pallas_call(
        _proj_step_kernel,
        grid=(1,),
        in_specs=[_full(a) for a in ins],
        out_specs=[pl.BlockSpec((n, w), lambda i: (0, 0)) for w in outs],
        out_shape=[jax.ShapeDtypeStruct((n, w), F32) for w in outs],
        compiler_params=_params(("arbitrary",)),
        name="proj_step",
    )(*ins)


CHUNK_FEATS = CMP_STRIDE * ROW_WIDTH
AB_WIDTH = 2 * 2 * 2 * V7X_LANES


def _compress_cols(get_cols, w_ref):
    outs = []
    for c in range(2):
        for kp in range(KV_HEADS // 2):
            base = c * KV_WIDTH + kp * V7X_LANES
            lhs = jnp.concatenate([get_cols(r, base).astype(BF16) for r in range(CMP_STRIDE)], axis=1)
            outs.append(_dot(lhs, w_ref[c]))
    return jnp.concatenate(outs, axis=1)


def _compress_rows_kernel(x_ref, w_ref, ab_ref):
    m = x_ref.shape[1] // CMP_STRIDE
    ab_ref[0] = _compress_cols(
        lambda r, lo: x_ref[lo // V7X_LANES, pl.ds(r, m, stride=CMP_STRIDE), :], w_ref)


def _compress_paged_kernel(pt_ref, pool_ref, perm_ref, w_ref, ab_ref, buf, lhs_s, sem):
    step = pl.program_id(0) * pl.num_programs(1) + pl.program_id(1)
    n_steps = pl.num_programs(0) * pl.num_programs(1)
    pages = buf.shape[1]
    slot = step % 2

    def copies(s, sl):
        return [pltpu.make_async_copy(pool_ref.at[pt_ref[s * pages + p]], buf.at[sl, p], sem.at[sl])
                for p in range(pages)]

    @pl.when(step == 0)
    def _():
        for cp in copies(0, 0):
            cp.start()

    @pl.when(step + 1 < n_steps)
    def _():
        for cp in copies(step + 1, 1 - slot):
            cp.start()

    for cp in copies(step, slot):
        cp.wait()

    pair_chunks = 2 * (PAGE_SIZE // CMP_STRIDE)
    unroll = math.gcd(pages // 2, REGROUP_UNROLL)

    def regroup(it, carry):
        tiles = []
        for u in range(unroll):
            pp = it * unroll + u
            for c in range(2):
                xt = jnp.concatenate([buf[slot, 2 * pp, c], buf[slot, 2 * pp + 1, c]], axis=1).astype(BF16)
                tiles.append((pp, c, _dot_nt(perm_ref[...], xt)))
        for pp, c, t in tiles:
            t = t.astype(BF16)
            m0 = pl.multiple_of(pp * pair_chunks, pair_chunks)
            for r in range(CMP_STRIDE):
                for kp in range(KV_HEADS // 2):
                    lhs_s[c * (KV_HEADS // 2) + kp, pl.ds(m0, pair_chunks), r * V7X_LANES:(r + 1) * V7X_LANES] = (
                        t[r * pair_chunks:(r + 1) * pair_chunks, kp * V7X_LANES:(kp + 1) * V7X_LANES])
        return carry

    lax.fori_loop(0, pages // 2 // unroll, regroup, 0)
    ab_ref[0] = jnp.concatenate(
        [_dot(lhs_s[g], w_ref[g // (KV_HEADS // 2)]) for g in range(lhs_s.shape[0])], axis=1)


def _compress_weights(cmp_pe, cmp_w1, cmp_w2):
    hid = cmp_w1.shape[2]
    w1r = cmp_w1.reshape(2, 2, CMP_STRIDE, HEAD_DIM, hid)
    eye = jnp.eye(2, dtype=cmp_w1.dtype)
    w = jnp.einsum("cardE,kl->crkdalE", w1r, eye)
    w = w.reshape(2, CMP_STRIDE * 2 * HEAD_DIM, 2 * 2 * hid).astype(BF16)
    pe = cmp_pe.reshape(2, 2, CMP_STRIDE, HEAD_DIM).transpose(1, 2, 0, 3)
    pe = jnp.broadcast_to(pe[:, :, :, None, :], (2, CMP_STRIDE, 2, KV_HEADS, HEAD_DIM))
    pe_rows = jnp.zeros((V7X_SUBLANES, CHUNK_FEATS), F32).at[0:2].set(pe.reshape(2, CHUNK_FEATS))
    pe_rows = pe_rows.reshape(V7X_SUBLANES * CMP_STRIDE, ROW_WIDTH // V7X_LANES, V7X_LANES).transpose(1, 0, 2)
    eye4 = jnp.eye(KV_HEADS, dtype=cmp_w2.dtype)
    w2k = jnp.einsum("ed,kl->keld", cmp_w2[0], eye4)
    w2k_pad = jnp.concatenate([w2k, jnp.zeros_like(w2k)], axis=3)
    w2k_pad = w2k_pad.reshape(KV_HEADS * hid, KV_HEADS * V7X_LANES).astype(BF16)
    w2k_bd = w2k.reshape(KV_HEADS * hid, KV_WIDTH).astype(BF16)
    w2v_bd = jnp.einsum("ed,kl->keld", cmp_w2[1], eye4).reshape(KV_HEADS * hid, KV_WIDTH).astype(BF16)
    return dict(w1=w, pe_rows=pe_rows, w2k_pad=w2k_pad, w2k_bd=w2k_bd, w2v_bd=w2v_bd,
                w2vt=w2v_bd.T)


def _compress_rows(x, w1, seq):
    groups, n, lanes = x.shape
    ns = n // seq
    nck = seq // CMP_STRIDE
    return pl.pallas_call(
        _compress_rows_kernel,
        grid=(ns,),
        in_specs=[pl.BlockSpec((groups, seq, lanes), lambda b: (0, b, 0)), _full(w1)],
        out_specs=pl.BlockSpec((1, nck, AB_WIDTH), lambda b: (b, 0, 0)),
        out_shape=jax.ShapeDtypeStruct((ns, nck, AB_WIDTH), F32),
        compiler_params=_params(("arbitrary",)),
        name="compress_rows",
    )(x, w1)


def _compress_paged(page_table, pool, w1):
    ns, n_pages = page_table.shape
    pages = min(CMP_PAGES_PER_STEP, n_pages)
    assert n_pages % pages == 0 and pages % 2 == 0
    assert pool.shape[1:] == (2, KV_WIDTH, PAGE_SIZE) and PAGE_SIZE == V7X_LANES
    steps = n_pages // pages
    chunks_per_page = PAGE_SIZE // CMP_STRIDE
    assert 2 * chunks_per_page == V7X_BF16_ROWS
    m = pages * chunks_per_page
    t = jnp.arange(2 * PAGE_SIZE)
    r, ch = t // (2 * chunks_per_page), t % (2 * chunks_per_page)
    src = (ch // chunks_per_page) * PAGE_SIZE + (ch % chunks_per_page) * CMP_STRIDE + r
    perm = (t[None, :] == src[:, None]).astype(BF16)
    return pl.pallas_call(
        _compress_paged_kernel,
        grid_spec=pltpu.PrefetchScalarGridSpec(
            num_scalar_prefetch=1,
            grid=(ns, steps),
            in_specs=[pl.BlockSpec(memory_space=pl.ANY),
                      pl.BlockSpec(perm.shape, lambda b, s, pt: (0, 0)),
                      pl.BlockSpec(w1.shape, lambda b, s, pt: (0, 0, 0))],
            out_specs=pl.BlockSpec((1, m, AB_WIDTH), lambda b, s, pt: (b, s, 0)),
            scratch_shapes=[pltpu.VMEM((2, pages) + pool.shape[1:], F32),
                            pltpu.VMEM((ROW_WIDTH // V7X_LANES, m, CMP_STRIDE * V7X_LANES), BF16),
                            pltpu.SemaphoreType.DMA((2,))],
        ),
        out_shape=jax.ShapeDtypeStruct((ns, n_pages * chunks_per_page, AB_WIDTH), F32),
        compiler_params=_params(("arbitrary", "arbitrary")),
        name="compress_paged",
    )(page_table.reshape(-1), pool, perm, w1)


def _cmp_hidden(ab, bias):
    n = ab.shape[0]
    hid = []
    for grp in range(4):
        base = grp * 2 * V7X_LANES
        a = ab[:, base:base + V7X_LANES] + bias[0:1, base:base + V7X_LANES]
        b = ab[:, base + V7X_LANES:base + 2 * V7X_LANES] + bias[1:2, base + V7X_LANES:base + 2 * V7X_LANES]
        hid.append(_silu(a + pltpu.roll(b, n - 1, 0)))
    return (jnp.concatenate(hid[0:2], axis=1).astype(BF16),
            jnp.concatenate(hid[2:4], axis=1).astype(BF16))


def _cmp_tokens_prompt_kernel(ab_ref, bias_ref, w2k_ref, w2vt_ref, kc_ref, vtc_ref):
    hk, hv = _cmp_hidden(ab_ref[0], bias_ref[0])
    k = _dot(hk, w2k_ref[...]).astype(BF16)
    vt = _dot_nt(w2vt_ref[...], hv).astype(BF16)
    for h in range(KV_HEADS):
        kc_ref[0, h] = k[:, h * V7X_LANES:(h + 1) * V7X_LANES]
        vtc_ref[0, h] = vt[h * HEAD_DIM:(h + 1) * HEAD_DIM]


def _cmp_tokens_prompt(ab, bias, cw):
    ns, nck, _ = ab.shape
    return pl.pallas_call(
        _cmp_tokens_prompt_kernel,
        grid=(ns,),
        in_specs=[pl.BlockSpec((1, nck, AB_WIDTH), lambda b: (b, 0, 0)),
                  _full(bias), _full(cw["w2k_pad"]), _full(cw["w2vt"])],
        out_specs=[pl.BlockSpec((1, KV_HEADS, nck, V7X_LANES), lambda b: (b, 0, 0, 0)),
                   pl.BlockSpec((1, KV_HEADS, HEAD_DIM, nck), lambda b: (b, 0, 0, 0))],
        out_shape=[jax.ShapeDtypeStruct((ns, KV_HEADS, nck, V7X_LANES), BF16),
                   jax.ShapeDtypeStruct((ns, KV_HEADS, HEAD_DIM, nck), BF16)],
        compiler_params=_params(("arbitrary",)),
        name="cmp_tokens_prompt",
    )(ab, bias, cw["w2k_pad"], cw["w2vt"])


def _topk_pick(score, k, axis):
    n = score.shape[axis]
    idx = lax.broadcasted_iota(jnp.int32, score.shape, axis).astype(F32)
    s = score
    for _ in range(k):
        m = jnp.max(s, axis=axis, keepdims=True)
        first = jnp.min(jnp.where(s == m, idx, float(n)), axis=axis, keepdims=True)
        s = jnp.where(idx == first, -jnp.inf, s)
        yield m, first.astype(jnp.int32)


def _topk_mask(score, k, axis):
    n = score.shape[axis]
    idx = lax.broadcasted_iota(jnp.int32, score.shape, axis).astype(F32)
    s = score
    for _ in range(k):
        m = jnp.max(s, axis=axis, keepdims=True)
        first = jnp.min(jnp.where(s == m, idx, float(n)), axis=axis, keepdims=True)
        s = jnp.where(idx == first, -jnp.inf, s)
    return (s == -jnp.inf) & (score > -jnp.inf)


def _block_scores(imp, blk, pos, n_blocks):
    valid = blk * SEL_BLOCK <= pos
    cur = pos // SEL_BLOCK
    forced = (blk == 0) | (blk == cur) | (blk == cur - 1)
    score = jnp.where(valid & forced, FORCE, jnp.where(valid, imp, -FORCE))
    return jnp.where(blk < n_blocks, score, -jnp.inf)


def _block_members(ci, blk, n_cmp, n_blocks):
    ratio = SEL_BLOCK // CMP_STRIDE
    return ((ci >= ratio * blk - (CMP_LEN // CMP_STRIDE - 1)) & (ci <= ratio * blk + ratio - 1)
            & (ci < n_cmp) & (blk < n_blocks))


def _softmax_tiles(m_ref, acc_ref, scores, vts, first):
    m_new = None if first else m_ref[...]
    for s in scores:
        t_max = jnp.max(s, axis=0, keepdims=True)
        m_new = t_max if m_new is None else jnp.maximum(m_new, t_max)
    m16 = m_new.astype(BF16)
    m_new = m16.astype(F32)
    acc = None if first else jnp.exp2(m_ref[...] - m_new) * acc_ref[...]
    for s, vt in zip(scores, vts):
        d = _dot(vt, jnp.exp2(s.astype(BF16) - m16))
        acc = d if acc is None else acc + d
    m_ref[...] = m_new
    acc_ref[...] = acc


def _softmax_finish(acc_ref):
    return acc_ref[0:HEAD_DIM, :] * (1.0 / jnp.maximum(acc_ref[HEAD_DIM:HEAD_DIM + 1, :], 1e-30))


def _attn_prompt_kernel(qt_ref, szt_ref, gt_ref, h1_ref, ksel_ref, vts_ref, kwin_ref, vtw_ref,
                        kc_ref, vtc_ref, wo_ref, gf_ref, y_ref, qa_s, ocmp_s, osz_s, m_s, acc_s, comb_s,
                        *, n_cmp, n_blocks):
    c = pl.program_id(1)
    tq = qt_ref.shape[1]
    ncp = kc_ref.shape[2]
    nbp = V7X_LANES - HEAD_DIM
    pos = c * tq + lax.broadcasted_iota(jnp.int32, (1, tq), 1)
    pos_rows = jnp.concatenate([pos] * GROUP, axis=1)

    kj = lax.broadcasted_iota(jnp.int32, (tq, tq), 0)
    qi = lax.broadcasted_iota(jnp.int32, (tq, tq), 1)
    causal = jnp.where(kj <= qi, 0.0, NEG)
    lower = jnp.where(kj >= qi, jnp.where(c >= 2, 0.0, NEG), NEG)
    has1 = jnp.where(c >= 1, 0.0, NEG)

    tok = lax.broadcasted_iota(jnp.int32, (ncp, 1), 0)
    cmp_mask = (CMP_STRIDE * tok + (CMP_LEN - 1) <= pos_rows) & (tok < n_cmp)
    blk = lax.broadcasted_iota(jnp.int32, (nbp, 1), 0)
    ci = lax.broadcasted_iota(jnp.int32, (1, ncp), 1)
    member_t = _block_members(ci, blk, n_cmp, n_blocks).astype(BF16)
    k_top = min(SEL_TOPK, n_blocks)

    for kvh in range(KV_HEADS):
        q_h = [qt_ref[(kvh * GROUP + g) * HEAD_DIM:(kvh * GROUP + g + 1) * HEAD_DIM, :] for g in range(GROUP)]
        q_t = jnp.concatenate(q_h, axis=1)
        qa0 = jnp.concatenate([q_t, jnp.zeros_like(q_t)], axis=0)
        sc = jnp.where(cmp_mask, _dot(kc_ref[0, kvh], qa0), NEG)
        e = jnp.exp2(sc - jnp.max(sc, axis=0, keepdims=True)) * cmp_mask.astype(F32)
        pc = e * (1.0 / jnp.maximum(jnp.sum(e, axis=0, keepdims=True), 1e-30))
        ocmp_s[kvh] = _dot(vtc_ref[0, kvh], pc.astype(BF16))
        psum = pc[:, 0:tq]
        for g in range(1, GROUP):
            psum = psum + pc[:, g * tq:(g + 1) * tq]
        imp = _dot(member_t, psum.astype(BF16))
        score = _block_scores(imp, blk, pos, n_blocks)
        picked = _topk_mask(score, k_top, 0) & (blk < n_blocks)
        mask_t = jnp.where(picked & (score > -0.5 * FORCE), 0.0, NEG).astype(BF16)
        for g in range(GROUP):
            qa_s[kvh * GROUP + g] = jnp.concatenate([q_h[g], mask_t], axis=0)

    def tiles(k_ref, v_ref, kts, biases, first=False):
        keys = [[k_ref[kt, kvh] for kt in kts] for kvh in range(KV_HEADS)]
        vts = [[v_ref[kt, kvh] for kt in kts] for kvh in range(KV_HEADS)]

        def qk(h):
            qa = qa_s[h]
            return [_dot(k, qa) if b is None else _dot(k, qa) + b for k, b in zip(keys[h // GROUP], biases)]

        scores = {h: qk(h) for h in range(QK_AHEAD)}
        for h in range(N_HEADS):
            _softmax_tiles(m_s.at[h], acc_s.at[h], scores.pop(h), vts[h // GROUP], first)
            if h + QK_AHEAD < N_HEADS:
                scores[h + QK_AHEAD] = qk(h + QK_AHEAD)

    gate = lambda h, br: gt_ref[h * N_BRANCH + br:h * N_BRANCH + br + 1, :]
    @pl.when(c % 2 == 1)
    def _():
        tiles(ksel_ref, vts_ref, (c - 1, c), (None, causal), first=True)

    @pl.when(c % 2 == 0)
    def _():
        tiles(ksel_ref, vts_ref, (c,), (causal,), first=True)

    def sel_pair(j, carry):
        tiles(ksel_ref, vts_ref, (2 * j, 2 * j + 1), (None, None))
        return carry

    lax.fori_loop(0, c // 2, sel_pair, 0)
    for h in range(N_HEADS):
        kvh, g = divmod(h, GROUP)
        comb_s[h] = gate(h, 0) * ocmp_s[kvh, :, g * tq:(g + 1) * tq] + gate(h, 1) * _softmax_finish(acc_s.at[h])
    tiles(kwin_ref, vtw_ref, (jnp.maximum(c - 2, 0), jnp.maximum(c - 1, 0), c), (lower, has1, causal), first=True)
    for h in range(N_HEADS):
        o = comb_s[h] + gate(h, 2) * _softmax_finish(acc_s.at[h])
        rows = slice(h * HEAD_DIM, (h + 1) * HEAD_DIM)
        osz_s[rows, :] = (o * szt_ref[rows, :].astype(F32)).astype(BF16)
    y = lax.dot_general(osz_s[...], wo_ref[...], _TN, preferred_element_type=F32)
    y_ref[...] = _rmsnorm(h1_ref[...] + y, gf_ref[...])


def _attn_prompt(qt, szt, gt, h1, ksel, vts, kwin, vtw, kc, vtc, wo, gf, batch, seq):
    n, d = h1.shape
    tq = Q_TILE
    nch = seq // tq
    d_attn = N_HEADS * HEAD_DIM
    n_cmp = seq // CMP_STRIDE - 1
    n_blocks = -(-seq // SEL_BLOCK)
    col = lambda r: pl.BlockSpec((r, tq), lambda b, c: (0, b * nch + c))
    seq_k = pl.BlockSpec((nch, KV_HEADS, tq, V7X_LANES), lambda b, c: (b, 0, 0, 0),
                         pipeline_mode=pl.Buffered(1))
    seq_v = pl.BlockSpec((nch, KV_HEADS, V_ROWS, tq), lambda b, c: (b, 0, 0, 0),
                         pipeline_mode=pl.Buffered(1))
    ncp = kc.shape[2]
    return pl.pallas_call(
        functools.partial(_attn_prompt_kernel, n_cmp=n_cmp, n_blocks=n_blocks),
        grid=(batch, nch),
        in_specs=[col(d_attn), col(d_attn), col(GATE_ROWS),
                  pl.BlockSpec((tq, d), lambda b, c: (b * nch + c, 0)),
                  seq_k, seq_v, seq_k, seq_v,
                  pl.BlockSpec((1, KV_HEADS, ncp, V7X_LANES), lambda b, c: (b, 0, 0, 0)),
                  pl.BlockSpec((1, KV_HEADS, HEAD_DIM, ncp), lambda b, c: (b, 0, 0, 0)),
                  _full(wo), _full(gf)],
        out_specs=pl.BlockSpec((tq, d), lambda b, c: (b * nch + c, 0)),
        out_shape=jax.ShapeDtypeStruct((n, d), F32),
        scratch_shapes=[pltpu.VMEM((N_HEADS, V7X_LANES, tq), BF16),
                        pltpu.VMEM((KV_HEADS, HEAD_DIM, GROUP * tq), F32),
                        pltpu.VMEM((d_attn, tq), BF16),
                        pltpu.VMEM((N_HEADS, 1, tq), F32),
                        pltpu.VMEM((N_HEADS, V_ROWS, tq), F32),
                        pltpu.VMEM((N_HEADS, HEAD_DIM, tq), F32)],
        compiler_params=_params(("arbitrary", "arbitrary")),
        name="attn_prompt",
    )(qt, szt, gt, h1, ksel, vts, kwin, vtw, kc, vtc, wo, gf)


HEAD_ROWS = KV_HEADS * V7X_SUBLANES


def _attn_step_cmp_kernel(ab_ref, bias_ref, w2k_ref, w2v_ref, qbd_ref, og_ref, idx_ref,
                          *, n_cmp, n_blocks, pos):
    nck = ab_ref.shape[1]
    nbp = idx_ref.shape[2] * (-(-n_blocks // idx_ref.shape[2]))
    tok = lax.broadcasted_iota(jnp.int32, (1, nck), 1)
    mask = (CMP_STRIDE * tok + (CMP_LEN - 1) <= pos) & (tok < n_cmp)
    r = lax.broadcasted_iota(jnp.int32, (HEAD_ROWS, 1), 0)
    lane = lax.broadcasted_iota(jnp.int32, (V7X_SUBLANES, KV_WIDTH), 1)
    r8 = lax.broadcasted_iota(jnp.int32, (V7X_SUBLANES, 1), 0)
    psum = jnp.zeros((V7X_SUBLANES, nck), F32)
    for i in range(ab_ref.shape[0]):
        hk, hv = _cmp_hidden(ab_ref[i], bias_ref[0])
        k = _dot(hk, w2k_ref[...]).astype(BF16)
        v = _dot(hv, w2v_ref[...]).astype(BF16)
        s = jnp.where(mask, _dot_nt(qbd_ref[i].astype(BF16), k), NEG)
        e = jnp.exp(s - jnp.max(s, axis=1, keepdims=True)) * mask.astype(F32)
        p = e * (1.0 / jnp.maximum(jnp.sum(e, axis=1, keepdims=True), 1e-30))
        p = jnp.where(r % V7X_SUBLANES < GROUP, p, 0.0)
        o = _dot(p.astype(BF16), v)
        og = jnp.zeros((V7X_SUBLANES, KV_WIDTH), F32)
        for h in range(KV_HEADS):
            og = jnp.where(lane // HEAD_DIM == h, o[h * V7X_SUBLANES:(h + 1) * V7X_SUBLANES], og)
        og_ref[i] = og
        for h in range(KV_HEADS):
            ph = jnp.sum(p[h * V7X_SUBLANES:(h + 1) * V7X_SUBLANES], axis=0, keepdims=True)
            psum = jnp.where(r8 == KV_HEADS * i + h, ph, psum)
    ci = lax.broadcasted_iota(jnp.int32, (nck, 1), 0)
    blk = lax.broadcasted_iota(jnp.int32, (1, nbp), 1)
    member = _block_members(ci, blk, n_cmp, n_blocks).astype(BF16)
    imp = _dot(psum.astype(BF16), member)
    score = _block_scores(imp, blk, pos, n_blocks)
    slot = lax.broadcasted_iota(jnp.int32, idx_ref.shape[1:], 1)
    picked = jnp.full(idx_ref.shape[1:], -1, jnp.int32)
    for j, (top_s, first) in enumerate(_topk_pick(score, min(SEL_TOPK, n_blocks), 1)):
        picked = jnp.where(slot == j, jnp.where(top_s > -0.5 * FORCE, first, -1), picked)
    idx_ref[0] = picked


def _attn_step_cmp(ab, bias, cw, qbd, n_cmp, n_blocks, pos):
    ns, nck, _ = ab.shape
    per_step = V7X_SUBLANES // KV_HEADS
    assert ns % per_step == 0
    og, idx = pl.pallas_call(
        functools.partial(_attn_step_cmp_kernel, n_cmp=n_cmp, n_blocks=n_blocks, pos=pos),
        grid=(ns // per_step,),
        in_specs=[pl.BlockSpec((per_step, nck, AB_WIDTH), lambda b: (b, 0, 0)),
                  _full(bias), _full(cw["w2k_bd"]), _full(cw["w2v_bd"]),
                  pl.BlockSpec((per_step,) + qbd.shape[1:], lambda b: (b, 0, 0))],
        out_specs=[pl.BlockSpec((per_step, V7X_SUBLANES, KV_WIDTH), lambda b: (b, 0, 0)),
                   pl.BlockSpec((1, V7X_SUBLANES, V7X_LANES), lambda b: (b, 0, 0))],
        out_shape=[jax.ShapeDtypeStruct((ns, V7X_SUBLANES, KV_WIDTH), F32),
                   jax.ShapeDtypeStruct((ns // per_step, V7X_SUBLANES, V7X_LANES), jnp.int32)],
        compiler_params=_params(("arbitrary",)),
        name="attn_step_cmp",
    )(ab, bias, cw["w2k_bd"], cw["w2v_bd"], qbd)
    return og, idx.reshape(ns, KV_HEADS, V7X_LANES)


def _attn_step_kernel(idx_ref, pt_ref, pool_ref, q_ref, new_ref, win_ref, osel_ref, owin_ref,
                      buf, sem, *, n_past_blocks, pos, win_pos0):
    b = pl.program_id(0)
    nb = pl.num_programs(0)
    per_head = buf.shape[4] // PAGE_SIZE
    n_slots = KV_HEADS * per_head
    pages_per_seq = pt_ref.shape[0] // nb
    halves = PAGE_SIZE // SEL_BLOCK
    slot = b % 2

    def block_of(bb, h, j):
        return idx_ref[bb * n_slots + h * per_head + j]

    def copies(bb, sl):
        out = []
        for h in range(KV_HEADS):
            for j in range(per_head):
                blk = jnp.clip(block_of(bb, h, j), 0, n_past_blocks - 1)
                page = pt_ref[bb * pages_per_seq + blk // halves]
                out.append(pltpu.make_async_copy(
                    pool_ref.at[page, :, h], buf.at[sl, h, :, :, pl.ds(j * PAGE_SIZE, PAGE_SIZE)], sem.at[sl]))
        return out

    @pl.when(b == 0)
    def _():
        for cp in copies(0, 0):
            cp.start()

    @pl.when(b + 1 < nb)
    def _():
        for cp in copies(b + 1, 1 - slot):
            cp.start()

    def with_new_token(q, s, vt16, k_new, v_new, new_ok):
        s_new = jnp.where(new_ok, jnp.sum(q * k_new, axis=1, keepdims=True), NEG)
        m = jnp.maximum(jnp.max(s, axis=1, keepdims=True), s_new)
        e = jnp.exp(s - m)
        e_new = jnp.where(new_ok, jnp.exp(s_new - m), 0.0)
        den = jnp.maximum(jnp.sum(e, axis=1, keepdims=True) + e_new, 1e-30)
        return (_dot_nt(e.astype(BF16), vt16) + e_new * v_new) * (1.0 / den)

    def new_row(branch, kv, h):
        r = (branch * 2 + kv) * KV_HEADS + h
        return new_ref[0, r:r + 1, :]

    n_win = win_ref.shape[4]
    wpos = win_pos0 + lax.broadcasted_iota(jnp.int32, (1, n_win), 1)
    wmask = (wpos >= 0) & (pos - wpos >= 0) & (pos - wpos <= WINDOW)
    for h in range(KV_HEADS):
        q = q_ref[0, h]
        s = jnp.where(wmask, _dot(q.astype(BF16), win_ref[0, 0, h].astype(BF16)), NEG)
        owin_ref[0, h] = with_new_token(q, s, win_ref[0, 1, h].astype(BF16),
                                        new_row(2, 0, h), new_row(2, 1, h), True)

    for cp in copies(b, slot):
        cp.wait()

    lane = lax.broadcasted_iota(jnp.int32, (1, per_head * PAGE_SIZE), 1)
    lane_slot = lane // PAGE_SIZE
    lane_half = (lane % PAGE_SIZE) // SEL_BLOCK
    for h in range(KV_HEADS):
        code = jnp.zeros((1, per_head * PAGE_SIZE), jnp.int32)
        has_new = jnp.int32(0)
        for j in range(per_head):
            blk = block_of(b, h, j)
            past = (blk >= 0) & (blk < n_past_blocks)
            code = jnp.where(lane_slot == j, jnp.where(past, 1 + blk % halves, 0), code)
            has_new = has_new | ((blk >= n_past_blocks) & (blk * SEL_BLOCK <= pos)).astype(jnp.int32)
        q = q_ref[0, h]
        s = jnp.where(code == 1 + lane_half, _dot(q.astype(BF16), buf[slot, h, 0].astype(BF16)), NEG)
        osel_ref[0, h] = with_new_token(q, s, buf[slot, h, 1].astype(BF16),
                                        new_row(1, 0, h), new_row(1, 1, h), has_new > 0)


def _attn_step(idx, page_table, pool_t, q8, new_rows, win_t, n_past_blocks, pos, win_pos0):
    ns = q8.shape[0]
    per_head = min(SEL_TOPK, n_past_blocks + 1)
    blk = lambda a: pl.BlockSpec((1,) + a.shape[1:], lambda b, i, p: (b,) + (0,) * (a.ndim - 1))
    out = pl.BlockSpec((1, KV_HEADS, V7X_SUBLANES, HEAD_DIM), lambda b, i, p: (b, 0, 0, 0))
    out_shape = jax.ShapeDtypeStruct((ns, KV_HEADS, V7X_SUBLANES, HEAD_DIM), F32)
    return pl.pallas_call(
        functools.partial(_attn_step_kernel, n_past_blocks=n_past_blocks, pos=pos, win_pos0=win_pos0),
        grid_spec=pltpu.PrefetchScalarGridSpec(
            num_scalar_prefetch=2,
            grid=(ns,),
            in_specs=[pl.BlockSpec(memory_space=pl.ANY), blk(q8), blk(new_rows), blk(win_t)],
            out_specs=[out, out],
            scratch_shapes=[pltpu.VMEM((2, KV_HEADS, 2, HEAD_DIM, per_head * PAGE_SIZE), F32),
                            pltpu.SemaphoreType.DMA((2,))],
        ),
        out_shape=[out_shape, out_shape],
        compiler_params=_params(("arbitrary",)),
        name="attn_step",
    )(idx, page_table.reshape(-1), pool_t, q8, new_rows, win_t)


def _out_step_kernel(oc_ref, os_ref, ow_ref, g0_ref, g1_ref, g2_ref, sz_ref, h1_ref, wo_ref, gf_ref, y_ref):
    o = g0_ref[...] * oc_ref[...] + g1_ref[...] * os_ref[...] + g2_ref[...] * ow_ref[...]
    y = _dot((o * sz_ref[...]).astype(BF16), wo_ref[...])
    y_ref[...] = _rmsnorm(h1_ref[...] + y, gf_ref[...])


def _out_step(*ins):
    h1 = ins[7]
    return pl.pallas_call(
        _out_step_kernel,
        grid=(1,),
        in_specs=[_full(a) for a in ins],
        out_specs=_full(h1),
        out_shape=jax.ShapeDtypeStruct(h1.shape, F32),
        compiler_params=_params(("arbitrary",)),
        name="out_step",
    )(*ins)


def kernel(x_prompt, x_sample, cache_conv, cache_cmp_kv, cache_sel_kv, cache_win_kv, page_table,
           norm_a, conv_w_in, conv_w, conv_w_out, norm_kv, w_kv, cmp_pe, cmp_w1, cmp_w2,
           norm_b, w_qg, w_o, norm_f):
    batch, seq, d = x_prompt.shape
    dec, dec_seq, _ = x_sample.shape
    assert norm_a.shape[0] == 1 and norm_b.shape[0] == 1 and dec_seq == 1
    assert cache_conv.shape[2] == CONV_WIDTH - 1 and cache_cmp_kv.shape[1] == PAGE_SIZE
    n_pool = cache_cmp_kv.shape[0]
    n_pages = page_table.shape[1]
    past_len = n_pages * PAGE_SIZE
    dc = conv_w.shape[2]
    d_attn = N_HEADS * HEAD_DIM

    win_r, wout = conv_w_in[0].astype(BF16), conv_w_out[0].astype(BF16)
    pw = _proj_weights(w_kv, w_qg[0])
    cw = _compress_weights(cmp_pe, cmp_w1, cmp_w2)
    g_a, g_kv, g_b, g_f = norm_a[0][None], norm_kv[None], norm_b[0][None], norm_f[None]
    wo16 = w_o[0].astype(BF16)
    bias = _compress_rows(cw["pe_rows"], cw["w1"], cw["pe_rows"].shape[1])
    to_tiles = lambda a: a.transpose(0, 2, 3, 4, 1)
    from_tiles = lambda a: a.reshape(a.shape[0], 2, KV_HEADS, HEAD_DIM, a.shape[2]).transpose(0, 4, 1, 2, 3)

    n = batch * seq
    h1_p, conv_p = _conv_prompt(x_prompt.reshape(n, d), g_a, win_r, conv_w[0], wout, seq)
    conv_p = conv_p.transpose(0, 2, 1, 3).reshape(batch, CONV_WIDTH - 1, dc)
    (kvt_c, kvt_s, kvt_w, kcmp, ksel, kwin, vts, vtw, qt, szt, gt) = _proj_prompt(
        h1_p, g_kv, g_b, pw, batch, seq)
    ab_p = _compress_rows(kcmp, cw["w1"], seq)
    kc, vtc = _cmp_tokens_prompt(ab_p, bias, cw)
    y_p = _attn_prompt(qt, szt, gt, h1_p, ksel, vts, kwin, vtw, kc, vtc, wo16, g_f, batch, seq)

    prev = cache_conv[0]
    h1_s, u_s = _conv_step(x_sample.reshape(dec, d), g_a, win_r, conv_w[0], wout, prev[:, 0], prev[:, 1])
    kv_s, q_s, sz_s, gate_s = _proj_step(h1_s, g_kv, g_b, pw)
    ab_s = _compress_paged(page_table, to_tiles(cache_cmp_kv).reshape(n_pool, 2, KV_WIDTH, PAGE_SIZE),
                           cw["w1"])
    n_cmp_s = (past_len + dec_seq) // CMP_STRIDE - 1
    assert n_cmp_s + 1 == ab_s.shape[1]
    n_blocks_s = -(-(past_len + dec_seq) // SEL_BLOCK)
    q8 = jnp.pad(q_s.reshape(dec, KV_HEADS, GROUP, HEAD_DIM),
                 ((0, 0), (0, 0), (0, V7X_SUBLANES - GROUP), (0, 0)))
    qbd = jnp.einsum("bkgd,kl->bkgld", q8, jnp.eye(KV_HEADS, dtype=F32)).reshape(dec, HEAD_ROWS, KV_WIDTH)
    ogc, idx = _attn_step_cmp(ab_s, bias, cw, qbd, n_cmp_s, n_blocks_s, past_len)
    k_sel = min(SEL_TOPK, n_blocks_s)
    idx = idx[:, :KV_HEADS, :k_sel].reshape(-1)
    n_win = cache_win_kv.shape[1]
    o_sel, o_win = _attn_step(idx, page_table, to_tiles(cache_sel_kv), q8,
                              kv_s.reshape(dec, N_BRANCH * 2 * KV_HEADS, HEAD_DIM), to_tiles(cache_win_kv),
                              past_len // SEL_BLOCK, past_len, past_len - n_win)
    heads = lambda o: o[:, :, :GROUP].reshape(dec, d_attn)
    o_cmp = ogc[:, :GROUP].reshape(dec, GROUP, KV_HEADS, HEAD_DIM).transpose(0, 2, 1, 3).reshape(dec, d_attn)
    gates = gate_s[:, :N_BRANCH * N_HEADS].reshape(dec, N_HEADS, N_BRANCH)
    g_br = [jnp.repeat(gates[:, :, br], HEAD_DIM, axis=1) for br in range(N_BRANCH)]
    y_s = _out_step(o_cmp, heads(o_sel), heads(o_win), g_br[0], g_br[1], g_br[2], sz_s, h1_s, wo16, g_f)

    kv5 = lambda a, lead: a.reshape(lead + (2, KV_HEADS, HEAD_DIM))
    keep_p = min(WINDOW, seq)
    kvs3 = kv_s.reshape(dec, N_BRANCH, ROW_WIDTH)
    win_all = jnp.concatenate([cache_win_kv, kv5(kvs3[:, 2], (dec, 1))], axis=1)
    keep_s = min(WINDOW, win_all.shape[1])
    conv_s = jnp.stack([prev[:, 1], u_s], axis=1)[None]
    return (y_p.reshape(batch, seq, d), y_s.reshape(dec, 1, d), conv_p[None], conv_s,
            from_tiles(kvt_c), kv5(kvs3[:, 0], (dec, 1)),
            from_tiles(kvt_s), kv5(kvs3[:, 1], (dec, 1)),
            from_tiles(kvt_w[:, :, seq - keep_p:]), win_all[:, win_all.shape[1] - keep_s:])
```
